```python
import jax, jax.numpy as jnp
from jax import lax
import numpy as np

D_MODEL = 1024
BATCH = 2
SEQ = 8192
DEPTH = 2

SGU_CHUNK = 128
SGU_GROUPS = 8
SGU_GROUP_DIM = 128
SGU_WIDTH = SGU_GROUPS * SGU_GROUP_DIM

GLA_HEADS = 4
GLA_DK = 128
GLA_DV = 256
GLA_KEY_WIDTH = GLA_HEADS * GLA_DK
GLA_VAL_WIDTH = GLA_HEADS * GLA_DV
GLA_GATE_RANK = 16
GLA_GATE_TEMP = 16.0
GLA_CHUNK = 64

FFN_HIDDEN = 2816
CONV_WIDTH = 3

EPS = 1e-6

IN_SIZES = (SGU_WIDTH, SGU_WIDTH, GLA_KEY_WIDTH, GLA_KEY_WIDTH, GLA_VAL_WIDTH, GLA_VAL_WIDTH,
            GLA_GATE_RANK, D_MODEL, D_MODEL)
IN_COLS = sum(IN_SIZES)
IN_SPLITS = tuple(int(c) for c in np.cumsum(IN_SIZES)[:-1])

kernel_name = "hybrid_sgu_gla_convffn"


def rmsnorm(x, g):
    xf = x.astype(jnp.float32)
    y = xf * lax.rsqrt(jnp.mean(xf * xf, axis=-1, keepdims=True) + EPS)
    return (y * g.astype(jnp.float32)).astype(x.dtype)


def layernorm(x, g, b):
    xf = x.astype(jnp.float32)
    mu = jnp.mean(xf, axis=-1, keepdims=True)
    var = jnp.mean(jnp.square(xf - mu), axis=-1, keepdims=True)
    y = (xf - mu) * lax.rsqrt(var + EPS)
    return (y * g.astype(jnp.float32) + b.astype(jnp.float32)).astype(x.dtype)


def sgu_branch(z_u, z_v, w_s, b_s, ln_g, ln_b):
    B, S, _ = z_u.shape
    n = S // SGU_CHUNK
    u = jax.nn.gelu(z_u)
    v = layernorm(jax.nn.gelu(z_v), ln_g, ln_b)
    v = v.reshape(B, n, SGU_CHUNK, SGU_GROUPS, SGU_GROUP_DIM)
    causal = jnp.tril(jnp.ones((SGU_CHUNK, SGU_CHUNK), dtype=bool))
    w = jnp.where(causal[None], w_s, 0).astype(v.dtype)
    sv = jnp.einsum('gts,bnsgc->bntgc', w, v) + b_s.T[None, None, :, :, None].astype(v.dtype)
    return u * sv.reshape(B, S, SGU_WIDTH)


def gla_branch(q, k, v, r, a_low, w_a2, b_a, norm_g):
    out_dtype = q.dtype
    B, S, _ = q.shape
    n = S // GLA_CHUNK
    f32 = jnp.float32
    log_a = jax.nn.log_sigmoid(a_low.astype(f32) @ w_a2.astype(f32) + b_a.astype(f32)) / GLA_GATE_TEMP

    def heads(t, d):
        return t.astype(f32).reshape(B, n, GLA_CHUNK, GLA_HEADS, d).transpose(1, 0, 3, 2, 4)

    qh = heads(q, GLA_DK) * (GLA_DK ** -0.5)
    kh = heads(k, GLA_DK)
    vh = heads(v, GLA_DV)
    bcum = jnp.cumsum(heads(log_a, GLA_DK), axis=3)
    b_last = bcum[:, :, :, -1:, :]
    q_e = qh * jnp.exp(bcum)
    k_intra = kh * jnp.exp(-bcum)
    k_state = kh * jnp.exp(b_last - bcum)
    decay_last = jnp.exp(b_last[:, :, :, 0, :])

    causal = jnp.tril(jnp.ones((GLA_CHUNK, GLA_CHUNK), dtype=bool))
    att = jnp.where(causal, jnp.einsum('nbhtd,nbhsd->nbhts', q_e, k_intra), 0.0)
    o_intra = jnp.einsum('nbhts,nbhsv->nbhtv', att, vh)

    def step(state, xs):
        q_c, k_c, v_c, dec = xs
        o = jnp.einsum('bhtd,bhdv->bhtv', q_c, state)
        state = state * dec[..., None] + jnp.einsum('bhsd,bhsv->bhdv', k_c, v_c)
        return state, o

    state0 = jnp.zeros((B, GLA_HEADS, GLA_DK, GLA_DV), f32)
    _, o_inter = lax.scan(step, state0, (q_e, k_state, vh, decay_last))
    o = rmsnorm(o_intra + o_inter, norm_g)
    o = o.transpose(1, 0, 3, 2, 4).reshape(B, S, GLA_VAL_WIDTH)
    return (o * jax.nn.silu(r.astype(f32))).astype(out_dtype)


def token_mixer(x, norm_g, w_in, w_a2, b_a, sgu_w, sgu_b, sgu_ln_g, sgu_ln_b, gla_norm_g,
                w_o_sgu, w_o_gla, w_out):
    h = rmsnorm(x, norm_g)
    proj = h @ w_in
    z_u, z_v, q, k, v, r, a_low, gate_sgu, gate_gla = jnp.split(proj, IN_SPLITS, axis=-1)
    y_sgu = sgu_branch(z_u, z_v, sgu_w, sgu_b, sgu_ln_g, sgu_ln_b) @ w_o_sgu
    y_gla = gla_branch(q, k, v, r, a_low, w_a2, b_a, gla_norm_g) @ w_o_gla
    merged = jax.nn.sigmoid(gate_sgu) * y_sgu + jax.nn.sigmoid(gate_gla) * y_gla
    return merged @ w_out


def conv_ffn(x, norm_g, w_up, conv_w, conv_b, w_down):
    S = x.shape[1]
    h = rmsnorm(x, norm_g) @ w_up
    hp = jnp.pad(h, ((0, 0), (CONV_WIDTH - 1, 0), (0, 0)))
    hc = sum(hp[:, j:j + S, :] * conv_w[j] for j in range(CONV_WIDTH)) + conv_b
    a, b = jnp.split(hc, 2, axis=-1)
    return (jax.nn.silu(a) * b) @ w_down


def setup_inputs(seed: int = 0) -> dict:
    key = jax.random.key(seed)
    ks = jax.random.split(key, 20)
    f32 = jnp.float32
    L = DEPTH

    def nrm(k, shape, scale):
        return jax.random.normal(k, shape, f32) * scale

    return {
        "x": nrm(ks[0], (BATCH, SEQ, D_MODEL), 1.0),
        "norm_mix_g": 1.0 + nrm(ks[1], (L, D_MODEL), 0.02),
        "w_in": nrm(ks[2], (L, D_MODEL, IN_COLS), D_MODEL ** -0.5),
        "w_a2": nrm(ks[3], (L, GLA_GATE_RANK, GLA_KEY_WIDTH), GLA_GATE_RANK ** -0.5),
        "b_a": nrm(ks[4], (L, GLA_KEY_WIDTH), 0.1),
        "sgu_w": nrm(ks[5], (L, SGU_GROUPS, SGU_CHUNK, SGU_CHUNK), 0.5 * SGU_CHUNK ** -0.5),
        "sgu_b": 1.0 + nrm(ks[6], (L, SGU_GROUPS, SGU_CHUNK), 0.02),
        "sgu_ln_g": 1.0 + nrm(ks[7], (L, SGU_WIDTH), 0.02),
        "sgu_ln_b": nrm(ks[8], (L, SGU_WIDTH), 0.02),
        "gla_norm_g": 1.0 + nrm(ks[9], (L, GLA_DV), 0.02),
        "w_o_sgu": nrm(ks[10], (L, SGU_WIDTH, D_MODEL), SGU_WIDTH ** -0.5),
        "w_o_gla": nrm(ks[11], (L, GLA_VAL_WIDTH, D_MODEL), GLA_VAL_WIDTH ** -0.5),
        "w_out": nrm(ks[12], (L, D_MODEL, D_MODEL), D_MODEL ** -0.5),
        "norm_ffn_g": 1.0 + nrm(ks[13], (L, D_MODEL), 0.02),
        "w_up": nrm(ks[14], (L, D_MODEL, 2 * FFN_HIDDEN), D_MODEL ** -0.5),
        "conv_w": nrm(ks[15], (L, CONV_WIDTH, 2 * FFN_HIDDEN), CONV_WIDTH ** -0.5),
        "conv_b": nrm(ks[16], (L, 2 * FFN_HIDDEN), 0.02),
        "w_down": nrm(ks[17], (L, FFN_HIDDEN, D_MODEL), FFN_HIDDEN ** -0.5),
        "final_norm_g": 1.0 + nrm(ks[18], (D_MODEL,), 0.02),
    }


def reference(x, norm_mix_g, w_in, w_a2, b_a, sgu_w, sgu_b, sgu_ln_g, sgu_ln_b, gla_norm_g,
              w_o_sgu, w_o_gla, w_out, norm_ffn_g, w_up, conv_w, conv_b, w_down, final_norm_g):
    for l in range(DEPTH):
        x = x + token_mixer(x, norm_mix_g[l], w_in[l], w_a2[l], b_a[l], sgu_w[l], sgu_b[l],
                            sgu_ln_g[l], sgu_ln_b[l], gla_norm_g[l], w_o_sgu[l], w_o_gla[l], w_out[l])
        x = x + conv_ffn(x, norm_ffn_g[l], w_up[l], conv_w[l], conv_b[l], w_down[l])
    return rmsnorm(x, final_norm_g)
```

```python
import functools

import jax
import jax.numpy as jnp
from jax import lax
from jax.experimental import pallas as pl
from jax.experimental.pallas import tpu as pltpu

D_MODEL = 1024
SGU_CHUNK = 128
SGU_GROUPS = 8
SGU_GROUP_DIM = 128
SGU_WIDTH = SGU_GROUPS * SGU_GROUP_DIM
GLA_HEADS = 4
GLA_DK = 128
GLA_DV = 256
GLA_KEY_WIDTH = GLA_HEADS * GLA_DK
GLA_VAL_WIDTH = GLA_HEADS * GLA_DV
GLA_GATE_RANK = 16
GLA_GATE_TEMP = 16.0
GLA_CHUNK = 64
FFN_HIDDEN = 2816
CONV_WIDTH = 3
EPS = 1e-6

LANES = 128
SUBLANES = 8
VMEM_LIMIT_BYTES = 56 * 1024 * 1024

OFF_ZU = 0
OFF_ZV = OFF_ZU + SGU_WIDTH
OFF_Q = OFF_ZV + SGU_WIDTH
OFF_K = OFF_Q + GLA_KEY_WIDTH
OFF_V = OFF_K + GLA_KEY_WIDTH
OFF_R = OFF_V + GLA_VAL_WIDTH
MAIN_COLS = OFF_R + GLA_VAL_WIDTH
REF_ALOW = MAIN_COLS
REF_GATES = REF_ALOW + GLA_GATE_RANK

MIXER_BLOCK = 256
FFN_BLOCK = 256

F32 = jnp.float32
BF16 = jnp.bfloat16


def _dot(a, b):
    return jnp.dot(a, b, preferred_element_type=F32)


def _rmsnorm(x, g):
    return x * lax.rsqrt(jnp.mean(x * x, axis=-1, keepdims=True) + EPS) * g


def _split3(x):
    hi = x.astype(BF16)
    r1 = x - hi.astype(F32)
    mid = r1.astype(BF16)
    lo = (r1 - mid.astype(F32)).astype(BF16)
    return hi, mid, lo


def _mixer_kernel(x_ref, g_ref, w_main_ref, w_alow_ref, w_gates_ref, w_a2_ref, b_a_ref,
                  sgu_w_ref, sgu_bias_ref, ln_g_ref, ln_b_ref, gnorm_ref,
                  w_o_sgu_ref, w_o_gla_ref, w_out_ref,
                  o_ref, state_ref, su_ref, og_ref):
    tb = x_ref.shape[0]
    n_sgu = tb // SGU_CHUNK
    n_gla = tb // GLA_CHUNK

    @pl.when(pl.program_id(1) == 0)
    def _():
        state_ref[...] = jnp.zeros_like(state_ref)

    x = x_ref[...]
    h = _rmsnorm(x, g_ref[...]).astype(BF16)

    def proj(off, width):
        return _dot(h, w_main_ref[:, off:off + width])

    u = jax.nn.gelu(proj(OFF_ZU, SGU_WIDTH))
    v = jax.nn.gelu(proj(OFF_ZV, SGU_WIDTH))
    mu = jnp.mean(v, axis=-1, keepdims=True)
    vc = v - mu
    var = jnp.mean(vc * vc, axis=-1, keepdims=True)
    vn = (vc * lax.rsqrt(var + EPS) * ln_g_ref[...] + ln_b_ref[...]).astype(BF16)

    row = lax.broadcasted_iota(jnp.int32, (SGU_CHUNK, SGU_CHUNK), 0)
    col = lax.broadcasted_iota(jnp.int32, (SGU_CHUNK, SGU_CHUNK), 1)
    causal = row >= col
    for g in range(SGU_GROUPS):
        cols = slice(g * SGU_GROUP_DIM, (g + 1) * SGU_GROUP_DIM)
        wg = jnp.where(causal, sgu_w_ref[g], 0.0).astype(BF16)
        vg = jnp.concatenate(
            [vn[c * SGU_CHUNK:(c + 1) * SGU_CHUNK, cols] for c in range(n_sgu)], axis=1)
        svg = _dot(wg, vg)
        bias = sgu_bias_ref[:, cols]
        for c in range(n_sgu):
            rows = slice(c * SGU_CHUNK, (c + 1) * SGU_CHUNK)
            sv = svg[:, c * SGU_GROUP_DIM:(c + 1) * SGU_GROUP_DIM] + bias
            su_ref[rows, cols] = (u[rows, cols] * sv).astype(BF16)
    y_sgu = _dot(su_ref[...], w_o_sgu_ref[...])

    a_low = _dot(h, w_alow_ref[...]).astype(BF16)
    z = _dot(a_low, w_a2_ref[...]) + b_a_ref[...]
    log_a = jax.nn.log_sigmoid(z) * (1.0 / GLA_GATE_TEMP)

    ri = lax.broadcasted_iota(jnp.int32, (tb, tb), 0)
    ci = lax.broadcasted_iota(jnp.int32, (tb, tb), 1)
    same_chunk = (ri // GLA_CHUNK) == (ci // GLA_CHUNK)
    lower = jnp.logical_and(same_chunk, ci <= ri)
    upper = jnp.logical_and(same_chunk, ci > ri)
    sum_mat = jnp.concatenate(
        [jnp.where(lower, 1.0, 0.0), jnp.where(upper, 1.0, 0.0)], axis=0).astype(BF16)
    parts = _dot(sum_mat, jnp.concatenate(_split3(log_a), axis=1))
    kw = GLA_KEY_WIDTH
    sums = parts[:, 0:kw] + parts[:, kw:2 * kw] + parts[:, 2 * kw:3 * kw]
    bcum = sums[0:tb]
    brev = sums[tb:2 * tb]

    q = proj(OFF_Q, GLA_KEY_WIDTH) * (GLA_DK ** -0.5)
    k = proj(OFF_K, GLA_KEY_WIDTH)
    q_e = (q * jnp.exp(bcum)).astype(BF16)
    k_intra = (k * jnp.exp(-bcum)).astype(BF16)
    k_state = (k * jnp.exp(brev)).astype(BF16)
    vv = proj(OFF_V, GLA_VAL_WIDTH).astype(BF16)
    r = proj(OFF_R, GLA_VAL_WIDTH)

    last_rows = [bcum[(c + 1) * GLA_CHUNK - 1:(c + 1) * GLA_CHUNK, :] for c in range(n_gla)]
    pad_rows = -n_gla % SUBLANES
    if pad_rows:
        last_rows.append(jnp.zeros((pad_rows, GLA_KEY_WIDTH), F32))
    decay = jnp.exp(jnp.concatenate(last_rows, axis=0))

    for hd in range(GLA_HEADS):
        kc = slice(hd * GLA_DK, (hd + 1) * GLA_DK)
        vcols = slice(hd * GLA_DV, (hd + 1) * GLA_DV)
        att = lax.dot_general(q_e[:, kc], k_intra[:, kc], (((1,), (1,)), ((), ())),
                              preferred_element_type=F32)
        att = jnp.where(lower, att, 0.0).astype(BF16)
        v_h = vv[:, vcols]
        o_intra = _dot(att, v_h)
        decay_t = decay[:, kc].T
        state = state_ref[hd]
        o_inter = []
        for c in range(n_gla):
            rows = slice(c * GLA_CHUNK, (c + 1) * GLA_CHUNK)
            o_inter.append(_dot(q_e[rows, kc], state.astype(BF16)))
            kv = lax.dot_general(k_state[rows, kc], v_h[rows], (((0,), (0,)), ((), ())),
                                 preferred_element_type=F32)
            state = state * decay_t[:, c:c + 1] + kv
        state_ref[hd] = state
        o = o_intra + jnp.concatenate(o_inter, axis=0)
        o = _rmsnorm(o, gnorm_ref[...])
        og_ref[:, vcols] = (o * jax.nn.silu(r[:, vcols])).astype(BF16)
    y_gla = _dot(og_ref[...], w_o_gla_ref[...])

    gate_sgu = jax.nn.sigmoid(_dot(h, w_gates_ref[:, 0:D_MODEL]))
    gate_gla = jax.nn.sigmoid(_dot(h, w_gates_ref[:, D_MODEL:2 * D_MODEL]))
    merged = (gate_sgu * y_sgu + gate_gla * y_gla).astype(BF16)
    o_ref[...] = x + _dot(merged, w_out_ref[...])


def _ffn_kernel(x_ref, g_ref, w_up_ref, conv_w_ref, conv_b_ref, w_down_ref, final_g_ref,
                o_ref, hbuf_ref, *, final_norm):
    tb = x_ref.shape[0]
    pad = SUBLANES

    @pl.when(pl.program_id(1) == 0)
    def _():
        hbuf_ref[0:pad, :] = jnp.zeros((pad, 2 * FFN_HIDDEN), F32)

    x = x_ref[...]
    hn = _rmsnorm(x, g_ref[...]).astype(BF16)
    hbuf_ref[pad:pad + tb, :] = _dot(hn, w_up_ref[...])
    hc = conv_b_ref[...]
    for j in range(CONV_WIDTH):
        shift = CONV_WIDTH - 1 - j
        hc = hc + hbuf_ref[pad - shift:pad - shift + tb, :] * conv_w_ref[j:j + 1, :]
    hbuf_ref[0:pad, :] = hbuf_ref[tb:tb + pad, :]
    a = hc[:, 0:FFN_HIDDEN]
    b = hc[:, FFN_HIDDEN:2 * FFN_HIDDEN]
    y = x + _dot((jax.nn.silu(a) * b).astype(BF16), w_down_ref[...])
    if final_norm:
        y = _rmsnorm(y, final_g_ref[...])
    o_ref[...] = y


def _resident(shape):
    zeros = (0,) * len(shape)
    return pl.BlockSpec(shape, lambda b, s: zeros, pipeline_mode=pl.Buffered(1))


def _token_block(tb):
    return pl.BlockSpec((None, tb, D_MODEL), lambda b, s: (b, s, 0))


def _mixer_call(x, weights):
    batch, seq, _ = x.shape
    tb = MIXER_BLOCK
    assert seq % tb == 0 and tb % SGU_CHUNK == 0 and tb % GLA_CHUNK == 0
    return pl.pallas_call(
        _mixer_kernel,
        grid=(batch, seq // tb),
        in_specs=[_token_block(tb)] + [_resident(w.shape) for w in weights],
        out_specs=_token_block(tb),
        out_shape=jax.ShapeDtypeStruct(x.shape, x.dtype),
        scratch_shapes=[
            pltpu.VMEM((GLA_HEADS, GLA_DK, GLA_DV), F32),
            pltpu.VMEM((tb, SGU_WIDTH), BF16),
            pltpu.VMEM((tb, GLA_VAL_WIDTH), BF16),
        ],
        compiler_params=pltpu.CompilerParams(
            dimension_semantics=("arbitrary", "arbitrary"),
            vmem_limit_bytes=VMEM_LIMIT_BYTES),
        name="mixer",
    )(x, *weights)


def _ffn_call(x, weights, final_norm):
    batch, seq, _ = x.shape
    tb = FFN_BLOCK
    assert seq % tb == 0
    return pl.pallas_call(
        functools.partial(_ffn_kernel, final_norm=final_norm),
        grid=(batch, seq // tb),
        in_specs=[_token_block(tb)] + [_resident(w.shape) for w in weights],
        out_specs=_token_block(tb),
        out_shape=jax.ShapeDtypeStruct(x.shape, x.dtype),
        scratch_shapes=[pltpu.VMEM((tb + SUBLANES, 2 * FFN_HIDDEN), F32)],
        compiler_params=pltpu.CompilerParams(
            dimension_semantics=("arbitrary", "arbitrary"),
            vmem_limit_bytes=VMEM_LIMIT_BYTES),
        name="ffn_final" if final_norm else "ffn",
    )(x, *weights)


def _row(v):
    return v.reshape(1, -1).astype(F32)


def kernel(x, norm_mix_g, w_in, w_a2, b_a, sgu_w, sgu_b, sgu_ln_g, sgu_ln_b, gla_norm_g,
           w_o_sgu, w_o_gla, w_out, norm_ffn_g, w_up, conv_w, conv_b, w_down, final_norm_g):
    depth = w_in.shape[0]
    for l in range(depth):
        w_l = w_in[l]
        w_alow = jnp.pad(w_l[:, REF_ALOW:REF_ALOW + GLA_GATE_RANK],
                         ((0, 0), (0, LANES - GLA_GATE_RANK)))
        w_a2p = jnp.pad(w_a2[l], ((0, LANES - GLA_GATE_RANK), (0, 0)))
        sgu_bias = jnp.repeat(sgu_b[l].T, SGU_GROUP_DIM, axis=1)
        mixer_weights = (
            _row(norm_mix_g[l]),
            w_l[:, :MAIN_COLS].astype(BF16),
            w_alow.astype(BF16),
            w_l[:, REF_GATES:].astype(BF16),
            w_a2p.astype(BF16),
            _row(b_a[l]),
            sgu_w[l],
            sgu_bias,
            _row(sgu_ln_g[l]),
            _row(sgu_ln_b[l]),
            _row(gla_norm_g[l]),
            w_o_sgu[l].astype(BF16),
            w_o_gla[l].astype(BF16),
            w_out[l].astype(BF16),
        )
        x = _mixer_call(x, mixer_weights)
        ffn_weights = (
            _row(norm_ffn_g[l]),
            w_up[l].astype(BF16),
            conv_w[l],
            _row(conv_b[l]),
            w_down[l].astype(BF16),
            _row(final_norm_g),
        )
        x = _ffn_call(x, ffn_weights, final_norm=(l == depth - 1))
    return x
```

```python
import functools

import jax
import jax.numpy as jnp
from jax import lax
from jax.experimental import pallas as pl
from jax.experimental.pallas import tpu as pltpu

D_MODEL = 1024
SGU_CHUNK = 128
SGU_GROUPS = 8
SGU_GROUP_DIM = 128
SGU_WIDTH = SGU_GROUPS * SGU_GROUP_DIM
GLA_HEADS = 4
GLA_DK = 128
GLA_DV = 256
GLA_KEY_WIDTH = GLA_HEADS * GLA_DK
GLA_VAL_WIDTH = GLA_HEADS * GLA_DV
GLA_GATE_RANK = 16
GLA_GATE_TEMP = 16.0
GLA_CHUNK = 64
FFN_HIDDEN = 2816
CONV_WIDTH = 3
EPS = 1e-6

LANES = 128
SUBLANES = 8
VMEM_LIMIT_BYTES = 56 * 1024 * 1024

OFF_ZU = 0
OFF_ZV = OFF_ZU + SGU_WIDTH
OFF_Q = OFF_ZV + SGU_WIDTH
OFF_K = OFF_Q + GLA_KEY_WIDTH
OFF_V = OFF_K + GLA_KEY_WIDTH
OFF_R = OFF_V + GLA_VAL_WIDTH
MAIN_COLS = OFF_R + GLA_VAL_WIDTH
REF_ALOW = MAIN_COLS
REF_GATES = REF_ALOW + GLA_GATE_RANK

MIXER_BLOCK = 256
FFN_BLOCK = 256

F32 = jnp.float32
BF16 = jnp.bfloat16


def _dot(a, b):
    return jnp.dot(a, b, preferred_element_type=F32)


def _rmsnorm(x, g):
    return x * lax.rsqrt(jnp.mean(x * x, axis=-1, keepdims=True) + EPS) * g


def _split3(x):
    hi = x.astype(BF16)
    r1 = x - hi.astype(F32)
    mid = r1.astype(BF16)
    lo = (r1 - mid.astype(F32)).astype(BF16)
    return hi, mid, lo


def _mixer_kernel(x_ref, g_ref, w_main_ref, w_alow_ref, w_gates_ref, w_a2_ref, b_a_ref,
                  sgu_w_ref, sgu_bias_ref, ln_g_ref, ln_b_ref, gnorm_ref,
                  w_o_sgu_ref, w_o_gla_ref, w_out_ref,
                  o_ref, state_ref, su_ref, og_ref):
    tb = x_ref.shape[0]
    n_sgu = tb // SGU_CHUNK
    n_gla = tb // GLA_CHUNK
    half = tb // 2
    assert n_gla == 4, "the block-level GLA recurrence is written for four chunks per block"

    @pl.when(pl.program_id(1) == 0)
    def _():
        state_ref[...] = jnp.zeros_like(state_ref)

    x = x_ref[...]
    h = _rmsnorm(x, g_ref[...]).astype(BF16)

    def proj(off, width):
        return _dot(h, w_main_ref[:, off:off + width])

    z_u = proj(OFF_ZU, SGU_WIDTH)
    z_v = proj(OFF_ZV, SGU_WIDTH)
    a_low = _dot(h, w_alow_ref[...]).astype(BF16)
    q = proj(OFF_Q, GLA_KEY_WIDTH) * (GLA_DK ** -0.5)
    k = proj(OFF_K, GLA_KEY_WIDTH)

    u = jax.nn.gelu(z_u)
    v = jax.nn.gelu(z_v)
    mu = jnp.mean(v, axis=-1, keepdims=True)
    vc = v - mu
    var = jnp.mean(vc * vc, axis=-1, keepdims=True)
    vn = (vc * lax.rsqrt(var + EPS) * ln_g_ref[...] + ln_b_ref[...]).astype(BF16)

    vv = proj(OFF_V, GLA_VAL_WIDTH).astype(BF16)
    r = proj(OFF_R, GLA_VAL_WIDTH)
    z = _dot(a_low, w_a2_ref[...]) + b_a_ref[...]
    log_a = jax.nn.log_sigmoid(z) * (1.0 / GLA_GATE_TEMP)

    row = lax.broadcasted_iota(jnp.int32, (SGU_CHUNK, SGU_CHUNK), 0)
    col = lax.broadcasted_iota(jnp.int32, (SGU_CHUNK, SGU_CHUNK), 1)
    causal = row >= col
    for g in range(SGU_GROUPS):
        cols = slice(g * SGU_GROUP_DIM, (g + 1) * SGU_GROUP_DIM)
        wg = jnp.where(causal, sgu_w_ref[g], 0.0).astype(BF16)
        vg = jnp.concatenate(
            [vn[c * SGU_CHUNK:(c + 1) * SGU_CHUNK, cols] for c in range(n_sgu)], axis=1)
        svg = _dot(wg, vg)
        bias = sgu_bias_ref[:, cols]
        for c in range(n_sgu):
            rows = slice(c * SGU_CHUNK, (c + 1) * SGU_CHUNK)
            sv = svg[:, c * SGU_GROUP_DIM:(c + 1) * SGU_GROUP_DIM] + bias
            su_ref[rows, cols] = (u[rows, cols] * sv).astype(BF16)

    ri = lax.broadcasted_iota(jnp.int32, (tb, tb), 0)
    ci = lax.broadcasted_iota(jnp.int32, (tb, tb), 1)
    chunk_gap = ri // GLA_CHUNK - ci // GLA_CHUNK
    intra = jnp.logical_and(chunk_gap == 0, ci <= ri)
    adjacent = chunk_gap == 1
    far = chunk_gap[half:, :half] >= 2
    kw = GLA_KEY_WIDTH
    parts = _dot(jnp.where(intra, 1.0, 0.0).astype(BF16),
                 jnp.concatenate(_split3(log_a), axis=1))
    bcum = parts[:, 0:kw] + parts[:, kw:2 * kw] + parts[:, 2 * kw:3 * kw]

    gate_sgu_pre = _dot(h, w_gates_ref[:, 0:D_MODEL])
    gate_gla_pre = _dot(h, w_gates_ref[:, D_MODEL:2 * D_MODEL])
    y_sgu = _dot(su_ref[...], w_o_sgu_ref[...])

    def per_chunk_rows(rows):
        return jnp.concatenate([jnp.broadcast_to(r, (GLA_CHUNK, kw)) for r in rows], axis=0)

    bl = [bcum[(c + 1) * GLA_CHUNK - 1:(c + 1) * GLA_CHUNK, :] for c in range(n_gla)]
    one = jnp.ones((1, kw), F32)
    c2 = bl[0] + bl[1]
    c3 = c2 + bl[2]
    c4 = c3 + bl[3]

    q_e32 = q * jnp.exp(bcum)
    k_state32 = k * jnp.exp(per_chunk_rows(bl) - bcum)
    q_e = q_e32.astype(BF16)
    k_intra = (k * jnp.exp(-bcum)).astype(BF16)
    k_state = k_state32.astype(BF16)
    q_blk = (q_e32 * per_chunk_rows(
        [one, jnp.exp(bl[0]), jnp.exp(c2), jnp.exp(c3)])).astype(BF16)
    k_blk = (k_state32 * per_chunk_rows(
        [jnp.exp(bl[1] + bl[2] + bl[3]), jnp.exp(bl[2] + bl[3]), jnp.exp(bl[3]), one])
             ).astype(BF16)
    q_far = (q_e32[half:] * per_chunk_rows([one, jnp.exp(bl[2])])).astype(BF16)
    k_far = (k_state32[:half] * per_chunk_rows([jnp.exp(bl[1]), one])).astype(BF16)
    block_decay = jnp.exp(jnp.concatenate([c4, jnp.zeros((SUBLANES - 1, kw), F32)], axis=0))

    def dot_nt(a, b):
        return lax.dot_general(a, b, (((1,), (1,)), ((), ())), preferred_element_type=F32)

    for hd in range(GLA_HEADS):
        kc = slice(hd * GLA_DK, (hd + 1) * GLA_DK)
        vcols = slice(hd * GLA_DV, (hd + 1) * GLA_DV)
        att = jnp.where(intra, dot_nt(q_e[:, kc], k_intra[:, kc]),
                        jnp.where(adjacent, dot_nt(q_e[:, kc], k_state[:, kc]), 0.0))
        att_far = jnp.where(far, dot_nt(q_far[:, kc], k_far[:, kc]), 0.0)
        att = jnp.concatenate(
            [att[:half],
             jnp.concatenate([att[half:, :half] + att_far, att[half:, half:]], axis=1)],
            axis=0).astype(BF16)
        v_h = vv[:, vcols]
        state = state_ref[hd]
        o = _dot(att, v_h) + _dot(q_blk[:, kc], state.astype(BF16))
        kv = lax.dot_general(k_blk[:, kc], v_h, (((0,), (0,)), ((), ())),
                             preferred_element_type=F32)
        state_ref[hd] = state * block_decay[:, kc].T[:, 0:1] + kv
        o = _rmsnorm(o, gnorm_ref[...])
        og_ref[:, vcols] = (o * jax.nn.silu(r[:, vcols])).astype(BF16)
    y_gla = _dot(og_ref[...], w_o_gla_ref[...])

    gate_sgu = jax.nn.sigmoid(gate_sgu_pre)
    gate_gla = jax.nn.sigmoid(gate_gla_pre)
    merged = (gate_sgu * y_sgu + gate_gla * y_gla).astype(BF16)
    o_ref[...] = x + _dot(merged, w_out_ref[...])


def _ffn_kernel(x_ref, g_ref, w_up_ref, conv_w_ref, conv_b_ref, w_down_ref, final_g_ref,
                o_ref, hbuf_ref, *, final_norm):
    tb = x_ref.shape[0]
    pad = SUBLANES

    @pl.when(pl.program_id(1) == 0)
    def _():
        hbuf_ref[0:pad, :] = jnp.zeros((pad, 2 * FFN_HIDDEN), F32)

    x = x_ref[...]
    hn = _rmsnorm(x, g_ref[...]).astype(BF16)
    hbuf_ref[pad:pad + tb, :] = _dot(hn, w_up_ref[...])
    hc = conv_b_ref[...]
    for j in range(CONV_WIDTH):
        shift = CONV_WIDTH - 1 - j
        hc = hc + hbuf_ref[pad - shift:pad - shift + tb, :] * conv_w_ref[j:j + 1, :]
    hbuf_ref[0:pad, :] = hbuf_ref[tb:tb + pad, :]
    a = hc[:, 0:FFN_HIDDEN]
    b = hc[:, FFN_HIDDEN:2 * FFN_HIDDEN]
    y = x + _dot((jax.nn.silu(a) * b).astype(BF16), w_down_ref[...])
    if final_norm:
        y = _rmsnorm(y, final_g_ref[...])
    o_ref[...] = y


def _resident(shape):
    zeros = (0,) * len(shape)
    return pl.BlockSpec(shape, lambda b, s: zeros, pipeline_mode=pl.Buffered(1))


def _token_block(tb):
    return pl.BlockSpec((None, tb, D_MODEL), lambda b, s: (b, s, 0))


def _mixer_call(x, weights):
    batch, seq, _ = x.shape
    tb = MIXER_BLOCK
    assert seq % tb == 0 and tb % SGU_CHUNK == 0 and tb % GLA_CHUNK == 0
    return pl.pallas_call(
        _mixer_kernel,
        grid=(batch, seq // tb),
        in_specs=[_token_block(tb)] + [_resident(w.shape) for w in weights],
        out_specs=_token_block(tb),
        out_shape=jax.ShapeDtypeStruct(x.shape, x.dtype),
        scratch_shapes=[
            pltpu.VMEM((GLA_HEADS, GLA_DK, GLA_DV), F32),
            pltpu.VMEM((tb, SGU_WIDTH), BF16),
            pltpu.VMEM((tb, GLA_VAL_WIDTH), BF16),
        ],
        compiler_params=pltpu.CompilerParams(
            dimension_semantics=("arbitrary", "arbitrary"),
            vmem_limit_bytes=VMEM_LIMIT_BYTES),
        name="mixer",
    )(x, *weights)


def _ffn_call(x, weights, final_norm):
    batch, seq, _ = x.shape
    tb = FFN_BLOCK
    assert seq % tb == 0
    return pl.pallas_call(
        functools.partial(_ffn_kernel, final_norm=final_norm),
        grid=(batch, seq // tb),
        in_specs=[_token_block(tb)] + [_resident(w.shape) for w in weights],
        out_specs=_token_block(tb),
        out_shape=jax.ShapeDtypeStruct(x.shape, x.dtype),
        scratch_shapes=[pltpu.VMEM((tb + SUBLANES, 2 * FFN_HIDDEN), F32)],
        compiler_params=pltpu.CompilerParams(
            dimension_semantics=("arbitrary", "arbitrary"),
            vmem_limit_bytes=VMEM_LIMIT_BYTES),
        name="ffn_final" if final_norm else "ffn",
    )(x, *weights)


def _row(v):
    return v.reshape(1, -1).astype(F32)


def kernel(x, norm_mix_g, w_in, w_a2, b_a, sgu_w, sgu_b, sgu_ln_g, sgu_ln_b, gla_norm_g,
           w_o_sgu, w_o_gla, w_out, norm_ffn_g, w_up, conv_w, conv_b, w_down, final_norm_g):
    depth = w_in.shape[0]
    for l in range(depth):
        w_l = w_in[l]
        w_alow = jnp.pad(w_l[:, REF_ALOW:REF_ALOW + GLA_GATE_RANK],
                         ((0, 0), (0, LANES - GLA_GATE_RANK)))
        w_a2p = jnp.pad(w_a2[l], ((0, LANES - GLA_GATE_RANK), (0, 0)))
        sgu_bias = jnp.repeat(sgu_b[l].T, SGU_GROUP_DIM, axis=1)
        mixer_weights = (
            _row(norm_mix_g[l]),
            w_l[:, :MAIN_COLS].astype(BF16),
            w_alow.astype(BF16),
            w_l[:, REF_GATES:].astype(BF16),
            w_a2p.astype(BF16),
            _row(b_a[l]),
            sgu_w[l],
            sgu_bias,
            _row(sgu_ln_g[l]),
            _row(sgu_ln_b[l]),
            _row(gla_norm_g[l]),
            w_o_sgu[l].astype(BF16),
            w_o_gla[l].astype(BF16),
            w_out[l].astype(BF16),
        )
        x = _mixer_call(x, mixer_weights)
        ffn_weights = (
            _row(norm_ffn_g[l]),
            w_up[l].astype(BF16),
            conv_w[l],
            _row(conv_b[l]),
            w_down[l].astype(BF16),
            _row(final_norm_g),
        )
        x = _ffn_call(x, ffn_weights, final_norm=(l == depth - 1))
    return x
```

```python
import functools

import jax
import jax.numpy as jnp
from jax import lax
from jax.experimental import pallas as pl
from jax.experimental.pallas import tpu as pltpu

D_MODEL = 1024
SGU_CHUNK = 128
SGU_GROUPS = 8
SGU_GROUP_DIM = 128
SGU_WIDTH = SGU_GROUPS * SGU_GROUP_DIM
GLA_HEADS = 4
GLA_DK = 128
GLA_DV = 256
GLA_KEY_WIDTH = GLA_HEADS * GLA_DK
GLA_VAL_WIDTH = GLA_HEADS * GLA_DV
GLA_GATE_RANK = 16
GLA_GATE_TEMP = 16.0
GLA_CHUNK = 64
FFN_HIDDEN = 2816
CONV_WIDTH = 3
EPS = 1e-6

LANES = 128
SUBLANES = 8
VMEM_LIMIT_BYTES = 56 * 1024 * 1024

OFF_ZU = 0
OFF_ZV = OFF_ZU + SGU_WIDTH
OFF_Q = OFF_ZV + SGU_WIDTH
OFF_K = OFF_Q + GLA_KEY_WIDTH
OFF_V = OFF_K + GLA_KEY_WIDTH
OFF_R = OFF_V + GLA_VAL_WIDTH
MAIN_COLS = OFF_R + GLA_VAL_WIDTH
IN_COLS = MAIN_COLS + GLA_GATE_RANK + 2 * D_MODEL
TAIL_ALIGNED = (IN_COLS - MAIN_COLS) // LANES * LANES
TAIL_COLS = TAIL_ALIGNED + LANES
GATE_SHIFT = GLA_GATE_RANK

MIXER_BLOCK = 256
FFN_BLOCK = 256
PREP_STEPS = 8

F32 = jnp.float32
BF16 = jnp.bfloat16


def _dot(a, b):
    return jnp.dot(a, b, preferred_element_type=F32)


def _rmsnorm(x, g):
    return x * lax.rsqrt(jnp.mean(x * x, axis=-1, keepdims=True) + EPS) * g


def _split3(x):
    hi = x.astype(BF16)
    r1 = x - hi.astype(F32)
    mid = r1.astype(BF16)
    lo = (r1 - mid.astype(F32)).astype(BF16)
    return hi, mid, lo


def _prep_kernel(w_in_ref, w_o_sgu_ref, w_o_gla_ref, w_out_ref, w_up_ref, w_down_ref,
                 main_ref, tail_ref, o_sgu_ref, o_gla_ref, out_ref, up_ref, down_ref):
    main_ref[...] = w_in_ref[:, 0:MAIN_COLS].astype(BF16)
    tail_ref[:, 0:TAIL_ALIGNED] = w_in_ref[:, MAIN_COLS:MAIN_COLS + TAIL_ALIGNED].astype(BF16)
    tail_ref[:, TAIL_ALIGNED:TAIL_COLS] = jnp.zeros((tail_ref.shape[0], LANES), BF16)
    rest = IN_COLS - MAIN_COLS - TAIL_ALIGNED
    tail_ref[:, TAIL_ALIGNED:TAIL_ALIGNED + rest] = (
        w_in_ref[:, MAIN_COLS + TAIL_ALIGNED:IN_COLS].astype(BF16))
    o_sgu_ref[...] = pltpu.roll(w_o_sgu_ref[...], GATE_SHIFT, axis=1).astype(BF16)
    o_gla_ref[...] = pltpu.roll(w_o_gla_ref[...], GATE_SHIFT, axis=1).astype(BF16)
    out_ref[...] = pltpu.roll(w_out_ref[...], GATE_SHIFT, axis=0).astype(BF16)
    up_ref[...] = w_up_ref[...].astype(BF16)
    down_ref[...] = w_down_ref[...].astype(BF16)


def _mixer_kernel(x_ref, g_ref, w_main_ref, w_tail_ref, w_a2_ref, b_a_ref,
                  sgu_w_ref, sgu_bias_ref, ln_g_ref, ln_b_ref, gnorm_ref,
                  w_o_sgu_ref, w_o_gla_ref, w_out_ref,
                  o_ref, state_ref, su_ref, og_ref):
    tb = x_ref.shape[0]
    n_sgu = tb // SGU_CHUNK
    n_gla = tb // GLA_CHUNK
    half = tb // 2
    assert n_gla == 4, "the block-level GLA recurrence is written for four chunks per block"

    @pl.when(pl.program_id(1) == 0)
    def _():
        state_ref[...] = jnp.zeros_like(state_ref)

    x = x_ref[...]
    h = _rmsnorm(x, g_ref[...]).astype(BF16)

    def proj(off, width):
        return _dot(h, w_main_ref[:, off:off + width])

    z_u = proj(OFF_ZU, SGU_WIDTH)
    z_v = proj(OFF_ZV, SGU_WIDTH)
    tail_head = _dot(h, w_tail_ref[:, 0:LANES])
    a_low = tail_head.astype(BF16)
    q = proj(OFF_Q, GLA_KEY_WIDTH) * (GLA_DK ** -0.5)
    k = proj(OFF_K, GLA_KEY_WIDTH)

    u = jax.nn.gelu(z_u)
    v = jax.nn.gelu(z_v)
    mu = jnp.mean(v, axis=-1, keepdims=True)
    vc = v - mu
    var = jnp.mean(vc * vc, axis=-1, keepdims=True)
    vn = (vc * lax.rsqrt(var + EPS) * ln_g_ref[...] + ln_b_ref[...]).astype(BF16)

    vv = proj(OFF_V, GLA_VAL_WIDTH).astype(BF16)
    r = proj(OFF_R, GLA_VAL_WIDTH)
    z = _dot(a_low, w_a2_ref[...]) + b_a_ref[...]
    log_a = jax.nn.log_sigmoid(z) * (1.0 / GLA_GATE_TEMP)

    row = lax.broadcasted_iota(jnp.int32, (SGU_CHUNK, SGU_CHUNK), 0)
    col = lax.broadcasted_iota(jnp.int32, (SGU_CHUNK, SGU_CHUNK), 1)
    causal = row >= col
    for g in range(SGU_GROUPS):
        cols = slice(g * SGU_GROUP_DIM, (g + 1) * SGU_GROUP_DIM)
        wg = jnp.where(causal, sgu_w_ref[g], 0.0).astype(BF16)
        vg = jnp.concatenate(
            [vn[c * SGU_CHUNK:(c + 1) * SGU_CHUNK, cols] for c in range(n_sgu)], axis=1)
        svg = _dot(wg, vg)
        bias = sgu_bias_ref[:, cols]
        for c in range(n_sgu):
            rows = slice(c * SGU_CHUNK, (c + 1) * SGU_CHUNK)
            sv = svg[:, c * SGU_GROUP_DIM:(c + 1) * SGU_GROUP_DIM] + bias
            su_ref[rows, cols] = (u[rows, cols] * sv).astype(BF16)

    ri = lax.broadcasted_iota(jnp.int32, (tb, tb), 0)
    ci = lax.broadcasted_iota(jnp.int32, (tb, tb), 1)
    chunk_gap = ri // GLA_CHUNK - ci // GLA_CHUNK
    intra = jnp.logical_and(chunk_gap == 0, ci <= ri)
    adjacent = chunk_gap == 1
    far = chunk_gap[half:, :half] >= 2
    kw = GLA_KEY_WIDTH
    parts = _dot(jnp.where(intra, 1.0, 0.0).astype(BF16),
                 jnp.concatenate(_split3(log_a), axis=1))
    bcum = parts[:, 0:kw] + parts[:, kw:2 * kw] + parts[:, 2 * kw:3 * kw]

    tail = jnp.concatenate([tail_head, _dot(h, w_tail_ref[:, LANES:TAIL_COLS])], axis=1)
    wrapped = lax.broadcasted_iota(jnp.int32, (tb, LANES), 1) < GATE_SHIFT

    def rotated_gate(base):
        first = jnp.where(wrapped, tail[:, base + D_MODEL:base + D_MODEL + LANES],
                          tail[:, base:base + LANES])
        return jnp.concatenate([first, tail[:, base + LANES:base + D_MODEL]], axis=1)

    gate_sgu_pre = rotated_gate(0)
    gate_gla_pre = rotated_gate(D_MODEL)
    y_sgu = _dot(su_ref[...], w_o_sgu_ref[...])

    def per_chunk_rows(rows):
        return jnp.concatenate([jnp.broadcast_to(r, (GLA_CHUNK, kw)) for r in rows], axis=0)

    bl = [bcum[(c + 1) * GLA_CHUNK - 1:(c + 1) * GLA_CHUNK, :] for c in range(n_gla)]
    one = jnp.ones((1, kw), F32)
    c2 = bl[0] + bl[1]
    c3 = c2 + bl[2]
    c4 = c3 + bl[3]

    q_e32 = q * jnp.exp(bcum)
    k_state32 = k * jnp.exp(per_chunk_rows(bl) - bcum)
    q_e = q_e32.astype(BF16)
    k_intra = (k * jnp.exp(-bcum)).astype(BF16)
    k_state = k_state32.astype(BF16)
    q_blk = (q_e32 * per_chunk_rows(
        [one, jnp.exp(bl[0]), jnp.exp(c2), jnp.exp(c3)])).astype(BF16)
    k_blk = (k_state32 * per_chunk_rows(
        [jnp.exp(bl[1] + bl[2] + bl[3]), jnp.exp(bl[2] + bl[3]), jnp.exp(bl[3]), one])
             ).astype(BF16)
    q_far = (q_e32[half:] * per_chunk_rows([one, jnp.exp(bl[2])])).astype(BF16)
    k_far = (k_state32[:half] * per_chunk_rows([jnp.exp(bl[1]), one])).astype(BF16)
    block_decay = jnp.exp(jnp.concatenate([c4, jnp.zeros((SUBLANES - 1, kw), F32)], axis=0))

    def dot_nt(a, b):
        return lax.dot_general(a, b, (((1,), (1,)), ((), ())), preferred_element_type=F32)

    for hd in range(GLA_HEADS):
        kc = slice(hd * GLA_DK, (hd + 1) * GLA_DK)
        vcols = slice(hd * GLA_DV, (hd + 1) * GLA_DV)
        att = jnp.where(intra, dot_nt(q_e[:, kc], k_intra[:, kc]),
                        jnp.where(adjacent, dot_nt(q_e[:, kc], k_state[:, kc]), 0.0))
        att_far = jnp.where(far, dot_nt(q_far[:, kc], k_far[:, kc]), 0.0)
        att = jnp.concatenate(
            [att[:half],
             jnp.concatenate([att[half:, :half] + att_far, att[half:, half:]], axis=1)],
            axis=0).astype(BF16)
        v_h = vv[:, vcols]
        state = state_ref[hd]
        o = _dot(att, v_h) + _dot(q_blk[:, kc], state.astype(BF16))
        kv = lax.dot_general(k_blk[:, kc], v_h, (((0,), (0,)), ((), ())),
                             preferred_element_type=F32)
        state_ref[hd] = state * block_decay[:, kc].T[:, 0:1] + kv
        o = _rmsnorm(o, gnorm_ref[...])
        og_ref[:, vcols] = (o * jax.nn.silu(r[:, vcols])).astype(BF16)
    y_gla = _dot(og_ref[...], w_o_gla_ref[...])

    gate_sgu = jax.nn.sigmoid(gate_sgu_pre)
    gate_gla = jax.nn.sigmoid(gate_gla_pre)
    merged = (gate_sgu * y_sgu + gate_gla * y_gla).astype(BF16)
    o_ref[...] = x + _dot(merged, w_out_ref[...])


def _ffn_kernel(x_ref, g_ref, w_up_ref, conv_w_ref, conv_b_ref, w_down_ref, final_g_ref,
                o_ref, hbuf_ref, *, final_norm):
    tb = x_ref.shape[0]
    pad = SUBLANES

    @pl.when(pl.program_id(1) == 0)
    def _():
        hbuf_ref[0:pad, :] = jnp.zeros((pad, 2 * FFN_HIDDEN), F32)

    x = x_ref[...]
    hn = _rmsnorm(x, g_ref[...]).astype(BF16)
    hbuf_ref[pad:pad + tb, :] = _dot(hn, w_up_ref[...])
    hc = conv_b_ref[...]
    for j in range(CONV_WIDTH):
        shift = CONV_WIDTH - 1 - j
        hc = hc + hbuf_ref[pad - shift:pad - shift + tb, :] * conv_w_ref[j:j + 1, :]
    hbuf_ref[0:pad, :] = hbuf_ref[tb:tb + pad, :]
    a = hc[:, 0:FFN_HIDDEN]
    b = hc[:, FFN_HIDDEN:2 * FFN_HIDDEN]
    y = x + _dot((jax.nn.silu(a) * b).astype(BF16), w_down_ref[...])
    if final_norm:
        y = _rmsnorm(y, final_g_ref[...])
    o_ref[...] = y


def _resident(shape):
    zeros = (0,) * len(shape)
    return pl.BlockSpec(shape, lambda b, s: zeros, pipeline_mode=pl.Buffered(1))


def _token_block(tb):
    return pl.BlockSpec((None, tb, D_MODEL), lambda b, s: (b, s, 0))


def _mixer_call(x, weights):
    batch, seq, _ = x.shape
    tb = MIXER_BLOCK
    assert seq % tb == 0 and tb % SGU_CHUNK == 0 and tb % GLA_CHUNK == 0
    return pl.pallas_call(
        _mixer_kernel,
        grid=(batch, seq // tb),
        in_specs=[_token_block(tb)] + [_resident(w.shape) for w in weights],
        out_specs=_token_block(tb),
        out_shape=jax.ShapeDtypeStruct(x.shape, x.dtype),
        scratch_shapes=[
            pltpu.VMEM((GLA_HEADS, GLA_DK, GLA_DV), F32),
            pltpu.VMEM((tb, SGU_WIDTH), BF16),
            pltpu.VMEM((tb, GLA_VAL_WIDTH), BF16),
        ],
        compiler_params=pltpu.CompilerParams(
            dimension_semantics=("arbitrary", "arbitrary"),
            vmem_limit_bytes=VMEM_LIMIT_BYTES),
        name="mixer",
    )(x, *weights)


def _ffn_call(x, weights, final_norm):
    batch, seq, _ = x.shape
    tb = FFN_BLOCK
    assert seq % tb == 0
    return pl.pallas_call(
        functools.partial(_ffn_kernel, final_norm=final_norm),
        grid=(batch, seq // tb),
        in_specs=[_token_block(tb)] + [_resident(w.shape) for w in weights],
        out_specs=_token_block(tb),
        out_shape=jax.ShapeDtypeStruct(x.shape, x.dtype),
        scratch_shapes=[pltpu.VMEM((tb + SUBLANES, 2 * FFN_HIDDEN), F32)],
        compiler_params=pltpu.CompilerParams(
            dimension_semantics=("arbitrary", "arbitrary"),
            vmem_limit_bytes=VMEM_LIMIT_BYTES),
        name="ffn_final" if final_norm else "ffn",
    )(x, *weights)


def _prep_call(layer, w_in, w_o_sgu, w_o_gla, w_out, w_up, w_down):
    n = PREP_STEPS

    def row_blocks(arr):
        rows, cols = arr.shape[1:]
        return (pl.BlockSpec((None, rows // n, cols), lambda i: (layer, i, 0)),
                pl.BlockSpec((rows // n, cols), lambda i: (i, 0)))

    ins, outs, shapes = [], [], []

    def add(in_spec, out_spec, shape):
        ins.append(in_spec)
        outs.append(out_spec)
        shapes.append(jax.ShapeDtypeStruct(shape, BF16))

    d = D_MODEL
    in_rows, _ = row_blocks(w_in)
    ins.append(in_rows)
    for cols in (MAIN_COLS, TAIL_COLS):
        outs.append(pl.BlockSpec((d // n, cols), lambda i: (i, 0)))
        shapes.append(jax.ShapeDtypeStruct((d, cols), BF16))
    add(*row_blocks(w_o_sgu), w_o_sgu.shape[1:])
    add(*row_blocks(w_o_gla), w_o_gla.shape[1:])
    add(pl.BlockSpec((None, d, d // n), lambda i: (layer, 0, i)),
        pl.BlockSpec((d, d // n), lambda i: (0, i)), w_out.shape[1:])
    add(*row_blocks(w_up), w_up.shape[1:])
    add(*row_blocks(w_down), w_down.shape[1:])
    return pl.pallas_call(
        _prep_kernel,
        grid=(n,),
        in_specs=ins,
        out_specs=outs,
        out_shape=shapes,
        compiler_params=pltpu.CompilerParams(
            dimension_semantics=("arbitrary",),
            vmem_limit_bytes=VMEM_LIMIT_BYTES),
        name="prep",
    )(w_in, w_o_sgu, w_o_gla, w_out, w_up, w_down)


def _row(v):
    return v.reshape(1, -1).astype(F32)


def kernel(x, norm_mix_g, w_in, w_a2, b_a, sgu_w, sgu_b, sgu_ln_g, sgu_ln_b, gla_norm_g,
           w_o_sgu, w_o_gla, w_out, norm_ffn_g, w_up, conv_w, conv_b, w_down, final_norm_g):
    depth = w_in.shape[0]
    assert w_in.shape[2] == IN_COLS
    for l in range(depth):
        w_main, w_tail, w_o_sgu_l, w_o_gla_l, w_out_l, w_up_l, w_down_l = _prep_call(
            l, w_in, w_o_sgu, w_o_gla, w_out, w_up, w_down)
        w_a2p = jnp.pad(w_a2[l], ((0, LANES - GLA_GATE_RANK), (0, 0)))
        sgu_bias = jnp.repeat(sgu_b[l].T, SGU_GROUP_DIM, axis=1)
        mixer_weights = (
            _row(norm_mix_g[l]),
            w_main,
            w_tail,
            w_a2p.astype(BF16),
            _row(b_a[l]),
            sgu_w[l],
            sgu_bias,
            _row(sgu_ln_g[l]),
            _row(sgu_ln_b[l]),
            _row(gla_norm_g[l]),
            w_o_sgu_l,
            w_o_gla_l,
            w_out_l,
        )
        x = _mixer_call(x, mixer_weights)
        ffn_weights = (
            _row(norm_ffn_g[l]),
            w_up_l,
            conv_w[l],
            _row(conv_b[l]),
            w_down_l,
            _row(final_norm_g),
        )
        x = _ffn_call(x, ffn_weights, final_norm=(l == depth - 1))
    return x
```

```python
import functools

import jax
import jax.numpy as jnp
from jax import lax
from jax.experimental import pallas as pl
from jax.experimental.pallas import tpu as pltpu

D_MODEL = 1024
SGU_CHUNK = 128
SGU_GROUPS = 8
SGU_GROUP_DIM = 128
SGU_WIDTH = SGU_GROUPS * SGU_GROUP_DIM
GLA_HEADS = 4
GLA_DK = 128
GLA_DV = 256
GLA_KEY_WIDTH = GLA_HEADS * GLA_DK
GLA_VAL_WIDTH = GLA_HEADS * GLA_DV
GLA_GATE_RANK = 16
GLA_GATE_TEMP = 16.0
GLA_CHUNK = 64
FFN_HIDDEN = 2816
CONV_WIDTH = 3
EPS = 1e-6

LANES = 128
SUBLANES = 8
VMEM_LIMIT_BYTES = 56 * 1024 * 1024

OFF_ZU = 0
OFF_ZV = OFF_ZU + SGU_WIDTH
OFF_Q = OFF_ZV + SGU_WIDTH
OFF_K = OFF_Q + GLA_KEY_WIDTH
OFF_V = OFF_K + GLA_KEY_WIDTH
OFF_R = OFF_V + GLA_VAL_WIDTH
MAIN_COLS = OFF_R + GLA_VAL_WIDTH
IN_COLS = MAIN_COLS + GLA_GATE_RANK + 2 * D_MODEL
TAIL_ALIGNED = (IN_COLS - MAIN_COLS) // LANES * LANES
TAIL_COLS = TAIL_ALIGNED + LANES
GATE_SHIFT = GLA_GATE_RANK

MIXER_BLOCK = 256
FFN_BLOCK = 256
PREP_STEPS = -(-IN_COLS // D_MODEL)
assert MAIN_COLS % D_MODEL == 0 and (PREP_STEPS - 1) * D_MODEL == MAIN_COLS + TAIL_ALIGNED

F32 = jnp.float32
BF16 = jnp.bfloat16


def _dot(a, b):
    return jnp.dot(a, b, preferred_element_type=F32)


def _rmsnorm(x, g):
    return x * lax.rsqrt(jnp.mean(x * x, axis=-1, keepdims=True) + EPS) * g


def _split3(x):
    hi = x.astype(BF16)
    r1 = x - hi.astype(F32)
    mid = r1.astype(BF16)
    lo = (r1 - mid.astype(F32)).astype(BF16)
    return hi, mid, lo


def _prep_kernel(w_in_t_ref, w_o_sgu_ref, w_o_gla_ref, w_out_ref, w_up_ref, w_down_ref,
                 main_ref, tail_ref, o_sgu_ref, o_gla_ref, out_ref, up_ref, down_ref):
    i = pl.program_id(0)
    n_main = MAIN_COLS // D_MODEL
    last = PREP_STEPS - 1

    @pl.when(i < n_main)
    def _():
        main_ref[...] = w_in_t_ref[...].T.astype(BF16)

    @pl.when(jnp.logical_and(i >= n_main, i < last))
    def _():
        tail_ref[...] = w_in_t_ref[...].T.astype(BF16)

    @pl.when(i == last)
    def _():
        rest = IN_COLS - last * D_MODEL
        rows = jnp.concatenate(
            [w_in_t_ref[0:rest, :], jnp.zeros((LANES - rest, D_MODEL), F32)], axis=0)
        tail_ref[:, 0:LANES] = rows.T.astype(BF16)

    o_sgu_ref[...] = pltpu.roll(w_o_sgu_ref[...], GATE_SHIFT, axis=1).astype(BF16)
    o_gla_ref[...] = pltpu.roll(w_o_gla_ref[...], GATE_SHIFT, axis=1).astype(BF16)
    out_ref[...] = pltpu.roll(w_out_ref[...], GATE_SHIFT, axis=0).astype(BF16)
    up_ref[...] = w_up_ref[...].astype(BF16)
    down_ref[...] = w_down_ref[...].astype(BF16)


def _mixer_kernel(x_ref, g_ref, w_main_ref, w_tail_ref, w_a2_ref, b_a_ref,
                  sgu_w_ref, sgu_bias_ref, ln_g_ref, ln_b_ref, gnorm_ref,
                  w_o_sgu_ref, w_o_gla_ref, w_out_ref,
                  o_ref, state_ref, su_ref, og_ref):
    tb = x_ref.shape[0]
    n_sgu = tb // SGU_CHUNK
    n_gla = tb // GLA_CHUNK
    half = tb // 2
    assert n_gla == 4, "the block-level GLA recurrence is written for four chunks per block"

    @pl.when(pl.program_id(1) == 0)
    def _():
        state_ref[...] = jnp.zeros_like(state_ref)

    x = x_ref[...]
    h = _rmsnorm(x, g_ref[...]).astype(BF16)

    def proj(off, width):
        return _dot(h, w_main_ref[:, off:off + width])

    z_u = proj(OFF_ZU, SGU_WIDTH)
    z_v = proj(OFF_ZV, SGU_WIDTH)
    tail_head = _dot(h, w_tail_ref[:, 0:LANES])
    a_low = tail_head.astype(BF16)
    q = proj(OFF_Q, GLA_KEY_WIDTH) * (GLA_DK ** -0.5)
    k = proj(OFF_K, GLA_KEY_WIDTH)

    u = jax.nn.gelu(z_u)
    v = jax.nn.gelu(z_v)
    mu = jnp.mean(v, axis=-1, keepdims=True)
    vc = v - mu
    var = jnp.mean(vc * vc, axis=-1, keepdims=True)
    vn = (vc * lax.rsqrt(var + EPS) * ln_g_ref[...] + ln_b_ref[...]).astype(BF16)

    vv = proj(OFF_V, GLA_VAL_WIDTH).astype(BF16)
    r = proj(OFF_R, GLA_VAL_WIDTH)
    z = _dot(a_low, w_a2_ref[...]) + b_a_ref[...]
    log_a = jax.nn.log_sigmoid(z) * (1.0 / GLA_GATE_TEMP)

    row = lax.broadcasted_iota(jnp.int32, (SGU_CHUNK, SGU_CHUNK), 0)
    col = lax.broadcasted_iota(jnp.int32, (SGU_CHUNK, SGU_CHUNK), 1)
    causal = row >= col
    for g in range(SGU_GROUPS):
        cols = slice(g * SGU_GROUP_DIM, (g + 1) * SGU_GROUP_DIM)
        wg = jnp.where(causal, sgu_w_ref[g], 0.0).astype(BF16)
        vg = jnp.concatenate(
            [vn[c * SGU_CHUNK:(c + 1) * SGU_CHUNK, cols] for c in range(n_sgu)], axis=1)
        svg = _dot(wg, vg)
        bias = sgu_bias_ref[:, cols]
        for c in range(n_sgu):
            rows = slice(c * SGU_CHUNK, (c + 1) * SGU_CHUNK)
            sv = svg[:, c * SGU_GROUP_DIM:(c + 1) * SGU_GROUP_DIM] + bias
            su_ref[rows, cols] = (u[rows, cols] * sv).astype(BF16)

    ri = lax.broadcasted_iota(jnp.int32, (tb, tb), 0)
    ci = lax.broadcasted_iota(jnp.int32, (tb, tb), 1)
    chunk_gap = ri // GLA_CHUNK - ci // GLA_CHUNK
    intra = jnp.logical_and(chunk_gap == 0, ci <= ri)
    adjacent = chunk_gap == 1
    far = chunk_gap[half:, :half] >= 2
    kw = GLA_KEY_WIDTH
    parts = _dot(jnp.where(intra, 1.0, 0.0).astype(BF16),
                 jnp.concatenate(_split3(log_a), axis=1))
    bcum = parts[:, 0:kw] + parts[:, kw:2 * kw] + parts[:, 2 * kw:3 * kw]

    tail = jnp.concatenate([tail_head, _dot(h, w_tail_ref[:, LANES:TAIL_COLS])], axis=1)
    wrapped = lax.broadcasted_iota(jnp.int32, (tb, LANES), 1) < GATE_SHIFT

    def rotated_gate(base):
        first = jnp.where(wrapped, tail[:, base + D_MODEL:base + D_MODEL + LANES],
                          tail[:, base:base + LANES])
        return jnp.concatenate([first, tail[:, base + LANES:base + D_MODEL]], axis=1)

    gate_sgu_pre = rotated_gate(0)
    gate_gla_pre = rotated_gate(D_MODEL)
    y_sgu = _dot(su_ref[...], w_o_sgu_ref[...])

    def per_chunk_rows(rows):
        return jnp.concatenate([jnp.broadcast_to(r, (GLA_CHUNK, kw)) for r in rows], axis=0)

    bl = [bcum[(c + 1) * GLA_CHUNK - 1:(c + 1) * GLA_CHUNK, :] for c in range(n_gla)]
    one = jnp.ones((1, kw), F32)
    c2 = bl[0] + bl[1]
    c3 = c2 + bl[2]
    c4 = c3 + bl[3]

    q_e32 = q * jnp.exp(bcum)
    k_state32 = k * jnp.exp(per_chunk_rows(bl) - bcum)
    q_e = q_e32.astype(BF16)
    k_intra = (k * jnp.exp(-bcum)).astype(BF16)
    k_state = k_state32.astype(BF16)
    q_blk = (q_e32 * per_chunk_rows(
        [one, jnp.exp(bl[0]), jnp.exp(c2), jnp.exp(c3)])).astype(BF16)
    k_blk = (k_state32 * per_chunk_rows(
        [jnp.exp(bl[1] + bl[2] + bl[3]), jnp.exp(bl[2] + bl[3]), jnp.exp(bl[3]), one])
             ).astype(BF16)
    q_far = (q_e32[half:] * per_chunk_rows([one, jnp.exp(bl[2])])).astype(BF16)
    k_far = (k_state32[:half] * per_chunk_rows([jnp.exp(bl[1]), one])).astype(BF16)
    block_decay = jnp.exp(jnp.concatenate([c4, jnp.zeros((SUBLANES - 1, kw), F32)], axis=0))

    def dot_nt(a, b):
        return lax.dot_general(a, b, (((1,), (1,)), ((), ())), preferred_element_type=F32)

    for hd in range(GLA_HEADS):
        kc = slice(hd * GLA_DK, (hd + 1) * GLA_DK)
        vcols = slice(hd * GLA_DV, (hd + 1) * GLA_DV)
        att = jnp.where(intra, dot_nt(q_e[:, kc], k_intra[:, kc]),
                        jnp.where(adjacent, dot_nt(q_e[:, kc], k_state[:, kc]), 0.0))
        att_far = jnp.where(far, dot_nt(q_far[:, kc], k_far[:, kc]), 0.0)
        att = jnp.concatenate(
            [att[:half],
             jnp.concatenate([att[half:, :half] + att_far, att[half:, half:]], axis=1)],
            axis=0).astype(BF16)
        v_h = vv[:, vcols]
        state = state_ref[hd]
        o = _dot(att, v_h) + _dot(q_blk[:, kc], state.astype(BF16))
        kv = lax.dot_general(k_blk[:, kc], v_h, (((0,), (0,)), ((), ())),
                             preferred_element_type=F32)
        state_ref[hd] = state * block_decay[:, kc].T[:, 0:1] + kv
        o = _rmsnorm(o, gnorm_ref[...])
        og_ref[:, vcols] = (o * jax.nn.silu(r[:, vcols])).astype(BF16)
    y_gla = _dot(og_ref[...], w_o_gla_ref[...])

    gate_sgu = jax.nn.sigmoid(gate_sgu_pre)
    gate_gla = jax.nn.sigmoid(gate_gla_pre)
    merged = (gate_sgu * y_sgu + gate_gla * y_gla).astype(BF16)
    o_ref[...] = x + _dot(merged, w_out_ref[...])


def _ffn_kernel(x_ref, g_ref, w_up_ref, conv_w_ref, conv_b_ref, w_down_ref, final_g_ref,
                o_ref, hbuf_ref, *, final_norm):
    tb = x_ref.shape[0]
    pad = SUBLANES

    @pl.when(pl.program_id(1) == 0)
    def _():
        hbuf_ref[0:pad, :] = jnp.zeros((pad, 2 * FFN_HIDDEN), F32)

    x = x_ref[...]
    hn = _rmsnorm(x, g_ref[...]).astype(BF16)
    hbuf_ref[pad:pad + tb, :] = _dot(hn, w_up_ref[...])
    hc = conv_b_ref[...]
    for j in range(CONV_WIDTH):
        shift = CONV_WIDTH - 1 - j
        hc = hc + hbuf_ref[pad - shift:pad - shift + tb, :] * conv_w_ref[j:j + 1, :]
    hbuf_ref[0:pad, :] = hbuf_ref[tb:tb + pad, :]
    a = hc[:, 0:FFN_HIDDEN]
    b = hc[:, FFN_HIDDEN:2 * FFN_HIDDEN]
    y = x + _dot((jax.nn.silu(a) * b).astype(BF16), w_down_ref[...])
    if final_norm:
        y = _rmsnorm(y, final_g_ref[...])
    o_ref[...] = y


def _resident(shape):
    zeros = (0,) * len(shape)
    return pl.BlockSpec(shape, lambda b, s: zeros, pipeline_mode=pl.Buffered(1))


def _token_block(tb):
    return pl.BlockSpec((None, tb, D_MODEL), lambda b, s: (b, s, 0))


def _mixer_call(x, weights):
    batch, seq, _ = x.shape
    tb = MIXER_BLOCK
    assert seq % tb == 0 and tb % SGU_CHUNK == 0 and tb % GLA_CHUNK == 0
    return pl.pallas_call(
        _mixer_kernel,
        grid=(batch, seq // tb),
        in_specs=[_token_block(tb)] + [_resident(w.shape) for w in weights],
        out_specs=_token_block(tb),
        out_shape=jax.ShapeDtypeStruct(x.shape, x.dtype),
        scratch_shapes=[
            pltpu.VMEM((GLA_HEADS, GLA_DK, GLA_DV), F32),
            pltpu.VMEM((tb, SGU_WIDTH), BF16),
            pltpu.VMEM((tb, GLA_VAL_WIDTH), BF16),
        ],
        compiler_params=pltpu.CompilerParams(
            dimension_semantics=("arbitrary", "arbitrary"),
            vmem_limit_bytes=VMEM_LIMIT_BYTES),
        name="mixer",
    )(x, *weights)


def _ffn_call(x, weights, final_norm):
    batch, seq, _ = x.shape
    tb = FFN_BLOCK
    assert seq % tb == 0
    return pl.pallas_call(
        functools.partial(_ffn_kernel, final_norm=final_norm),
        grid=(batch, seq // tb),
        in_specs=[_token_block(tb)] + [_resident(w.shape) for w in weights],
        out_specs=_token_block(tb),
        out_shape=jax.ShapeDtypeStruct(x.shape, x.dtype),
        scratch_shapes=[pltpu.VMEM((tb + SUBLANES, 2 * FFN_HIDDEN), F32)],
        compiler_params=pltpu.CompilerParams(
            dimension_semantics=("arbitrary", "arbitrary"),
            vmem_limit_bytes=VMEM_LIMIT_BYTES),
        name="ffn_final" if final_norm else "ffn",
    )(x, *weights)


def _prep_call(layer, w_in_t, w_o_sgu, w_o_gla, w_out, w_up, w_down):
    n = PREP_STEPS

    def row_blocks(arr):
        rows, cols = arr.shape[1:]
        return (pl.BlockSpec((None, rows // n, cols), lambda i: (layer, i, 0)),
                pl.BlockSpec((rows // n, cols), lambda i: (i, 0)))

    ins, outs, shapes = [], [], []

    def add(in_spec, out_spec, shape):
        ins.append(in_spec)
        outs.append(out_spec)
        shapes.append(jax.ShapeDtypeStruct(shape, BF16))

    d = D_MODEL
    n_main = MAIN_COLS // d
    ins.append(pl.BlockSpec((None, d, d), lambda i: (layer, i, 0)))
    outs.append(pl.BlockSpec((d, d), lambda i: (0, jnp.minimum(i, n_main - 1))))
    shapes.append(jax.ShapeDtypeStruct((d, MAIN_COLS), BF16))
    outs.append(pl.BlockSpec((d, d), lambda i: (0, jnp.maximum(i - n_main, 0))))
    shapes.append(jax.ShapeDtypeStruct((d, TAIL_COLS), BF16))
    add(*row_blocks(w_o_sgu), w_o_sgu.shape[1:])
    add(*row_blocks(w_o_gla), w_o_gla.shape[1:])
    add(pl.BlockSpec((None, d, d // n), lambda i: (layer, 0, i)),
        pl.BlockSpec((d, d // n), lambda i: (0, i)), w_out.shape[1:])
    add(*row_blocks(w_up), w_up.shape[1:])
    add(*row_blocks(w_down), w_down.shape[1:])
    return pl.pallas_call(
        _prep_kernel,
        grid=(n,),
        in_specs=ins,
        out_specs=outs,
        out_shape=shapes,
        compiler_params=pltpu.CompilerParams(
            dimension_semantics=("arbitrary",),
            vmem_limit_bytes=VMEM_LIMIT_BYTES),
        name="prep",
    )(w_in_t, w_o_sgu, w_o_gla, w_out, w_up, w_down)


def _row(v):
    return v.reshape(1, -1).astype(F32)


def kernel(x, norm_mix_g, w_in, w_a2, b_a, sgu_w, sgu_b, sgu_ln_g, sgu_ln_b, gla_norm_g,
           w_o_sgu, w_o_gla, w_out, norm_ffn_g, w_up, conv_w, conv_b, w_down, final_norm_g):
    depth = w_in.shape[0]
    assert w_in.shape[2] == IN_COLS
    w_in_t = jnp.swapaxes(w_in, 1, 2)
    for l in range(depth):
        w_main, w_tail, w_o_sgu_l, w_o_gla_l, w_out_l, w_up_l, w_down_l = _prep_call(
            l, w_in_t, w_o_sgu, w_o_gla, w_out, w_up, w_down)
        w_a2p = jnp.pad(w_a2[l], ((0, LANES - GLA_GATE_RANK), (0, 0)))
        sgu_bias = jnp.repeat(sgu_b[l].T, SGU_GROUP_DIM, axis=1)
        mixer_weights = (
            _row(norm_mix_g[l]),
            w_main,
            w_tail,
            w_a2p.astype(BF16),
            _row(b_a[l]),
            sgu_w[l],
            sgu_bias,
            _row(sgu_ln_g[l]),
            _row(sgu_ln_b[l]),
            _row(gla_norm_g[l]),
            w_o_sgu_l,
            w_o_gla_l,
            w_out_l,
        )
        x = _mixer_call(x, mixer_weights)
        ffn_weights = (
            _row(norm_ffn_g[l]),
            w_up_l,
            conv_w[l],
            _row(conv_b[l]),
            w_down_l,
            _row(final_norm_g),
        )
        x = _ffn_call(x, ffn_weights, final_norm=(l == depth - 1))
    return x
```

```python
import functools

import jax
import jax.numpy as jnp
from jax import lax
from jax.experimental import pallas as pl
from jax.experimental.pallas import tpu as pltpu

D_MODEL = 1024
SGU_CHUNK = 128
SGU_GROUPS = 8
SGU_GROUP_DIM = 128
SGU_WIDTH = SGU_GROUPS * SGU_GROUP_DIM
GLA_HEADS = 4
GLA_DK = 128
GLA_DV = 256
GLA_KEY_WIDTH = GLA_HEADS * GLA_DK
GLA_VAL_WIDTH = GLA_HEADS * GLA_DV
GLA_GATE_RANK = 16
GLA_GATE_TEMP = 16.0
GLA_CHUNK = 64
FFN_HIDDEN = 2816
CONV_WIDTH = 3
EPS = 1e-6

LANES = 128
SUBLANES = 8
VMEM_LIMIT_BYTES = 56 * 1024 * 1024

OFF_ZU = 0
OFF_ZV = OFF_ZU + SGU_WIDTH
OFF_Q = OFF_ZV + SGU_WIDTH
OFF_K = OFF_Q + GLA_KEY_WIDTH
OFF_V = OFF_K + GLA_KEY_WIDTH
OFF_R = OFF_V + GLA_VAL_WIDTH
MAIN_COLS = OFF_R + GLA_VAL_WIDTH
IN_COLS = MAIN_COLS + GLA_GATE_RANK + 2 * D_MODEL
TAIL_ALIGNED = (IN_COLS - MAIN_COLS) // LANES * LANES
TAIL_COLS = TAIL_ALIGNED + LANES
GATE_SHIFT = GLA_GATE_RANK

MIXER_BLOCK = 512
FFN_BLOCK = 512
GLA_BLOCK_CHUNKS = 4
PREP_STEPS = -(-IN_COLS // D_MODEL)
assert MAIN_COLS % D_MODEL == 0 and (PREP_STEPS - 1) * D_MODEL == MAIN_COLS + TAIL_ALIGNED

F32 = jnp.float32
BF16 = jnp.bfloat16


def _dot(a, b):
    return jnp.dot(a, b, preferred_element_type=F32)


def _rmsnorm(x, g):
    return x * lax.rsqrt(jnp.mean(x * x, axis=-1, keepdims=True) + EPS) * g


def _split3(x):
    hi = x.astype(BF16)
    r1 = x - hi.astype(F32)
    mid = r1.astype(BF16)
    lo = (r1 - mid.astype(F32)).astype(BF16)
    return hi, mid, lo


def _prep_kernel(w_in_t_ref, w_o_sgu_ref, w_o_gla_ref, w_out_ref, w_up_ref, w_down_ref,
                 main_ref, tail_ref, o_sgu_ref, o_gla_ref, out_ref, up_ref, down_ref):
    i = pl.program_id(0)
    n_main = MAIN_COLS // D_MODEL
    last = PREP_STEPS - 1

    @pl.when(i < n_main)
    def _():
        main_ref[...] = w_in_t_ref[...].T.astype(BF16)

    @pl.when(jnp.logical_and(i >= n_main, i < last))
    def _():
        tail_ref[...] = w_in_t_ref[...].T.astype(BF16)

    @pl.when(i == last)
    def _():
        rest = IN_COLS - last * D_MODEL
        rows = jnp.concatenate(
            [w_in_t_ref[0:rest, :], jnp.zeros((LANES - rest, D_MODEL), F32)], axis=0)
        tail_ref[:, 0:LANES] = rows.T.astype(BF16)

    o_sgu_ref[...] = pltpu.roll(w_o_sgu_ref[...], GATE_SHIFT, axis=1).astype(BF16)
    o_gla_ref[...] = pltpu.roll(w_o_gla_ref[...], GATE_SHIFT, axis=1).astype(BF16)
    out_ref[...] = pltpu.roll(w_out_ref[...], GATE_SHIFT, axis=0).astype(BF16)
    up_ref[...] = w_up_ref[...].astype(BF16)
    down_ref[...] = w_down_ref[...].astype(BF16)


def _mixer_kernel(x_ref, g_ref, w_main_ref, w_tail_ref, w_a2_ref, b_a_ref,
                  sgu_w_ref, sgu_bias_ref, ln_g_ref, ln_b_ref, gnorm_ref,
                  w_o_sgu_ref, w_o_gla_ref, w_out_ref,
                  o_ref, state_ref, su_ref, og_ref):
    tb = x_ref.shape[0]
    n_sgu = tb // SGU_CHUNK
    gb = GLA_BLOCK_CHUNKS * GLA_CHUNK
    n_blk = tb // gb
    half = gb // 2

    @pl.when(pl.program_id(1) == 0)
    def _():
        state_ref[...] = jnp.zeros_like(state_ref)

    x = x_ref[...]
    h = _rmsnorm(x, g_ref[...]).astype(BF16)

    def proj(off, width):
        return _dot(h, w_main_ref[:, off:off + width])

    z_u = proj(OFF_ZU, SGU_WIDTH)
    z_v = proj(OFF_ZV, SGU_WIDTH)
    tail_head = _dot(h, w_tail_ref[:, 0:LANES])
    a_low = tail_head.astype(BF16)
    q = proj(OFF_Q, GLA_KEY_WIDTH) * (GLA_DK ** -0.5)
    k = proj(OFF_K, GLA_KEY_WIDTH)

    u = jax.nn.gelu(z_u)
    v = jax.nn.gelu(z_v)
    mu = jnp.mean(v, axis=-1, keepdims=True)
    vc = v - mu
    var = jnp.mean(vc * vc, axis=-1, keepdims=True)
    vn = (vc * lax.rsqrt(var + EPS) * ln_g_ref[...] + ln_b_ref[...]).astype(BF16)

    vv = proj(OFF_V, GLA_VAL_WIDTH).astype(BF16)
    r = proj(OFF_R, GLA_VAL_WIDTH)
    z = _dot(a_low, w_a2_ref[...]) + b_a_ref[...]
    log_a = jax.nn.log_sigmoid(z) * (1.0 / GLA_GATE_TEMP)

    row = lax.broadcasted_iota(jnp.int32, (SGU_CHUNK, SGU_CHUNK), 0)
    col = lax.broadcasted_iota(jnp.int32, (SGU_CHUNK, SGU_CHUNK), 1)
    causal = row >= col
    for g in range(SGU_GROUPS):
        cols = slice(g * SGU_GROUP_DIM, (g + 1) * SGU_GROUP_DIM)
        wg = jnp.where(causal, sgu_w_ref[g], 0.0).astype(BF16)
        vg = jnp.concatenate(
            [vn[c * SGU_CHUNK:(c + 1) * SGU_CHUNK, cols] for c in range(n_sgu)], axis=1)
        svg = _dot(wg, vg)
        bias = sgu_bias_ref[:, cols]
        for c in range(n_sgu):
            rows = slice(c * SGU_CHUNK, (c + 1) * SGU_CHUNK)
            sv = svg[:, c * SGU_GROUP_DIM:(c + 1) * SGU_GROUP_DIM] + bias
            su_ref[rows, cols] = (u[rows, cols] * sv).astype(BF16)

    ri = lax.broadcasted_iota(jnp.int32, (gb, gb), 0)
    ci = lax.broadcasted_iota(jnp.int32, (gb, gb), 1)
    chunk_gap = ri // GLA_CHUNK - ci // GLA_CHUNK
    intra = jnp.logical_and(chunk_gap == 0, ci <= ri)
    adjacent = chunk_gap == 1
    far = chunk_gap[half:, :half] >= 2
    kw = GLA_KEY_WIDTH
    prefix_sum = jnp.where(intra, 1.0, 0.0).astype(BF16)
    log_a_split = jnp.concatenate(_split3(log_a), axis=1)
    bcums = []
    for blk in range(n_blk):
        parts = _dot(prefix_sum, log_a_split[blk * gb:(blk + 1) * gb])
        bcums.append(parts[:, 0:kw] + parts[:, kw:2 * kw] + parts[:, 2 * kw:3 * kw])

    tail = jnp.concatenate([tail_head, _dot(h, w_tail_ref[:, LANES:TAIL_COLS])], axis=1)
    wrapped = lax.broadcasted_iota(jnp.int32, (tb, LANES), 1) < GATE_SHIFT

    def rotated_gate(base):
        first = jnp.where(wrapped, tail[:, base + D_MODEL:base + D_MODEL + LANES],
                          tail[:, base:base + LANES])
        return jnp.concatenate([first, tail[:, base + LANES:base + D_MODEL]], axis=1)

    gate_sgu_pre = rotated_gate(0)
    gate_gla_pre = rotated_gate(D_MODEL)
    y_sgu = _dot(su_ref[...], w_o_sgu_ref[...])

    def per_chunk_rows(rows):
        return jnp.concatenate([jnp.broadcast_to(r, (GLA_CHUNK, kw)) for r in rows], axis=0)

    def dot_nt(a, b):
        return lax.dot_general(a, b, (((1,), (1,)), ((), ())), preferred_element_type=F32)

    one = jnp.ones((1, kw), F32)
    for blk in range(n_blk):
        rows = slice(blk * gb, (blk + 1) * gb)
        bcum = bcums[blk]
        q_b = q[rows]
        k_b = k[rows]
        bl = [bcum[(c + 1) * GLA_CHUNK - 1:(c + 1) * GLA_CHUNK, :] for c in range(GLA_BLOCK_CHUNKS)]
        c2 = bl[0] + bl[1]
        c3 = c2 + bl[2]
        c4 = c3 + bl[3]

        q_e32 = q_b * jnp.exp(bcum)
        k_state32 = k_b * jnp.exp(per_chunk_rows(bl) - bcum)
        q_e = q_e32.astype(BF16)
        k_intra = (k_b * jnp.exp(-bcum)).astype(BF16)
        k_state = k_state32.astype(BF16)
        q_blk = (q_e32 * per_chunk_rows(
            [one, jnp.exp(bl[0]), jnp.exp(c2), jnp.exp(c3)])).astype(BF16)
        k_blk = (k_state32 * per_chunk_rows(
            [jnp.exp(bl[1] + bl[2] + bl[3]), jnp.exp(bl[2] + bl[3]), jnp.exp(bl[3]), one])
                 ).astype(BF16)
        q_far = (q_e32[half:] * per_chunk_rows([one, jnp.exp(bl[2])])).astype(BF16)
        k_far = (k_state32[:half] * per_chunk_rows([jnp.exp(bl[1]), one])).astype(BF16)
        block_decay = jnp.exp(
            jnp.concatenate([c4, jnp.zeros((SUBLANES - 1, kw), F32)], axis=0))

        for hd in range(GLA_HEADS):
            kc = slice(hd * GLA_DK, (hd + 1) * GLA_DK)
            vcols = slice(hd * GLA_DV, (hd + 1) * GLA_DV)
            att = jnp.where(intra, dot_nt(q_e[:, kc], k_intra[:, kc]),
                            jnp.where(adjacent, dot_nt(q_e[:, kc], k_state[:, kc]), 0.0))
            att_far = jnp.where(far, dot_nt(q_far[:, kc], k_far[:, kc]), 0.0)
            att = jnp.concatenate(
                [att[:half],
                 jnp.concatenate([att[half:, :half] + att_far, att[half:, half:]], axis=1)],
                axis=0).astype(BF16)
            v_h = vv[rows, vcols]
            state = state_ref[hd]
            o = _dot(att, v_h) + _dot(q_blk[:, kc], state.astype(BF16))
            kv = lax.dot_general(k_blk[:, kc], v_h, (((0,), (0,)), ((), ())),
                                 preferred_element_type=F32)
            state_ref[hd] = state * block_decay[:, kc].T[:, 0:1] + kv
            o = _rmsnorm(o, gnorm_ref[...])
            og_ref[rows, vcols] = (o * jax.nn.silu(r[rows, vcols])).astype(BF16)
    y_gla = _dot(og_ref[...], w_o_gla_ref[...])

    gate_sgu = jax.nn.sigmoid(gate_sgu_pre)
    gate_gla = jax.nn.sigmoid(gate_gla_pre)
    merged = (gate_sgu * y_sgu + gate_gla * y_gla).astype(BF16)
    o_ref[...] = x + _dot(merged, w_out_ref[...])


def _ffn_kernel(x_ref, g_ref, w_up_ref, conv_w_ref, conv_b_ref, w_down_ref, final_g_ref,
                o_ref, hbuf_ref, *, final_norm):
    tb = x_ref.shape[0]
    pad = SUBLANES

    @pl.when(pl.program_id(1) == 0)
    def _():
        hbuf_ref[0:pad, :] = jnp.zeros((pad, 2 * FFN_HIDDEN), F32)

    x = x_ref[...]
    hn = _rmsnorm(x, g_ref[...]).astype(BF16)
    hbuf_ref[pad:pad + tb, :] = _dot(hn, w_up_ref[...])
    hc = conv_b_ref[...]
    for j in range(CONV_WIDTH):
        shift = CONV_WIDTH - 1 - j
        hc = hc + hbuf_ref[pad - shift:pad - shift + tb, :] * conv_w_ref[j:j + 1, :]
    hbuf_ref[0:pad, :] = hbuf_ref[tb:tb + pad, :]
    a = hc[:, 0:FFN_HIDDEN]
    b = hc[:, FFN_HIDDEN:2 * FFN_HIDDEN]
    y = x + _dot((jax.nn.silu(a) * b).astype(BF16), w_down_ref[...])
    if final_norm:
        y = _rmsnorm(y, final_g_ref[...])
    o_ref[...] = y


def _resident(shape):
    zeros = (0,) * len(shape)
    return pl.BlockSpec(shape, lambda b, s: zeros, pipeline_mode=pl.Buffered(1))


def _token_block(tb):
    return pl.BlockSpec((None, tb, D_MODEL), lambda b, s: (b, s, 0))


def _mixer_call(x, weights):
    batch, seq, _ = x.shape
    tb = MIXER_BLOCK
    assert seq % tb == 0 and tb % SGU_CHUNK == 0 and tb % (GLA_BLOCK_CHUNKS * GLA_CHUNK) == 0
    return pl.pallas_call(
        _mixer_kernel,
        grid=(batch, seq // tb),
        in_specs=[_token_block(tb)] + [_resident(w.shape) for w in weights],
        out_specs=_token_block(tb),
        out_shape=jax.ShapeDtypeStruct(x.shape, x.dtype),
        scratch_shapes=[
            pltpu.VMEM((GLA_HEADS, GLA_DK, GLA_DV), F32),
            pltpu.VMEM((tb, SGU_WIDTH), BF16),
            pltpu.VMEM((tb, GLA_VAL_WIDTH), BF16),
        ],
        compiler_params=pltpu.CompilerParams(
            dimension_semantics=("arbitrary", "arbitrary"),
            vmem_limit_bytes=VMEM_LIMIT_BYTES),
        name="mixer",
    )(x, *weights)


def _ffn_call(x, weights, final_norm):
    batch, seq, _ = x.shape
    tb = FFN_BLOCK
    assert seq % tb == 0
    return pl.pallas_call(
        functools.partial(_ffn_kernel, final_norm=final_norm),
        grid=(batch, seq // tb),
        in_specs=[_token_block(tb)] + [_resident(w.shape) for w in weights],
        out_specs=_token_block(tb),
        out_shape=jax.ShapeDtypeStruct(x.shape, x.dtype),
        scratch_shapes=[pltpu.VMEM((tb + SUBLANES, 2 * FFN_HIDDEN), F32)],
        compiler_params=pltpu.CompilerParams(
            dimension_semantics=("arbitrary", "arbitrary"),
            vmem_limit_bytes=VMEM_LIMIT_BYTES),
        name="ffn_final" if final_norm else "ffn",
    )(x, *weights)


def _prep_call(layer, w_in_t, w_o_sgu, w_o_gla, w_out, w_up, w_down):
    n = PREP_STEPS

    def row_blocks(arr):
        rows, cols = arr.shape[1:]
        return (pl.BlockSpec((None, rows // n, cols), lambda i: (layer, i, 0)),
                pl.BlockSpec((rows // n, cols), lambda i: (i, 0)))

    ins, outs, shapes = [], [], []

    def add(in_spec, out_spec, shape):
        ins.append(in_spec)
        outs.append(out_spec)
        shapes.append(jax.ShapeDtypeStruct(shape, BF16))

    d = D_MODEL
    n_main = MAIN_COLS // d
    ins.append(pl.BlockSpec((None, d, d), lambda i: (layer, i, 0)))
    outs.append(pl.BlockSpec((d, d), lambda i: (0, jnp.minimum(i, n_main - 1))))
    shapes.append(jax.ShapeDtypeStruct((d, MAIN_COLS), BF16))
    outs.append(pl.BlockSpec((d, d), lambda i: (0, jnp.maximum(i - n_main, 0))))
    shapes.append(jax.ShapeDtypeStruct((d, TAIL_COLS), BF16))
    add(*row_blocks(w_o_sgu), w_o_sgu.shape[1:])
    add(*row_blocks(w_o_gla), w_o_gla.shape[1:])
    add(pl.BlockSpec((None, d, d // n), lambda i: (layer, 0, i)),
        pl.BlockSpec((d, d // n), lambda i: (0, i)), w_out.shape[1:])
    add(*row_blocks(w_up), w_up.shape[1:])
    add(*row_blocks(w_down), w_down.shape[1:])
    return pl.pallas_call(
        _prep_kernel,
        grid=(n,),
        in_specs=ins,
        out_specs=outs,
        out_shape=shapes,
        compiler_params=pltpu.CompilerParams(
            dimension_semantics=("arbitrary",),
            vmem_limit_bytes=VMEM_LIMIT_BYTES),
        name="prep",
    )(w_in_t, w_o_sgu, w_o_gla, w_out, w_up, w_down)


def _row(v):
    return v.reshape(1, -1).astype(F32)


def kernel(x, norm_mix_g, w_in, w_a2, b_a, sgu_w, sgu_b, sgu_ln_g, sgu_ln_b, gla_norm_g,
           w_o_sgu, w_o_gla, w_out, norm_ffn_g, w_up, conv_w, conv_b, w_down, final_norm_g):
    depth = w_in.shape[0]
    assert w_in.shape[2] == IN_COLS
    w_in_t = jnp.swapaxes(w_in, 1, 2)
    for l in range(depth):
        w_main, w_tail, w_o_sgu_l, w_o_gla_l, w_out_l, w_up_l, w_down_l = _prep_call(
            l, w_in_t, w_o_sgu, w_o_gla, w_out, w_up, w_down)
        w_a2p = jnp.pad(w_a2[l], ((0, LANES - GLA_GATE_RANK), (0, 0)))
        sgu_bias = jnp.repeat(sgu_b[l].T, SGU_GROUP_DIM, axis=1)
        mixer_weights = (
            _row(norm_mix_g[l]),
            w_main,
            w_tail,
            w_a2p.astype(BF16),
            _row(b_a[l]),
            sgu_w[l],
            sgu_bias,
            _row(sgu_ln_g[l]),
            _row(sgu_ln_b[l]),
            _row(gla_norm_g[l]),
            w_o_sgu_l,
            w_o_gla_l,
            w_out_l,
        )
        x = _mixer_call(x, mixer_weights)
        ffn_weights = (
            _row(norm_ffn_g[l]),
            w_up_l,
            conv_w[l],
            _row(conv_b[l]),
            w_down_l,
            _row(final_norm_g),
        )
        x = _ffn_call(x, ffn_weights, final_norm=(l == depth - 1))
    return x
```

```python
import functools

import jax
import jax.numpy as jnp
from jax import lax
from jax.experimental import pallas as pl
from jax.experimental.pallas import tpu as pltpu

D_MODEL = 1024
SGU_CHUNK = 128
SGU_GROUPS = 8
SGU_GROUP_DIM = 128
SGU_WIDTH = SGU_GROUPS * SGU_GROUP_DIM
GLA_HEADS = 4
GLA_DK = 128
GLA_DV = 256
GLA_KEY_WIDTH = GLA_HEADS * GLA_DK
GLA_VAL_WIDTH = GLA_HEADS * GLA_DV
GLA_GATE_RANK = 16
GLA_GATE_TEMP = 16.0
GLA_CHUNK = 64
FFN_HIDDEN = 2816
CONV_WIDTH = 3
EPS = 1e-6

LANES = 128
SUBLANES = 8
BF16_ROWS = 16
VMEM_LIMIT_BYTES = 56 * 1024 * 1024

OFF_ZU = 0
OFF_ZV = OFF_ZU + SGU_WIDTH
OFF_Q = OFF_ZV + SGU_WIDTH
OFF_K = OFF_Q + GLA_KEY_WIDTH
OFF_V = OFF_K + GLA_KEY_WIDTH
OFF_R = OFF_V + GLA_VAL_WIDTH
MAIN_COLS = OFF_R + GLA_VAL_WIDTH
IN_COLS = MAIN_COLS + GLA_GATE_RANK + 2 * D_MODEL
TAIL_ALIGNED = (IN_COLS - MAIN_COLS) // LANES * LANES
TAIL_COLS = TAIL_ALIGNED + LANES
GATE_SHIFT = GLA_GATE_RANK

MIXER_BLOCK = 512
FFN_BLOCK = 512
GLA_BLOCK_CHUNKS = 4
PREP_STEPS = -(-IN_COLS // D_MODEL)
assert MAIN_COLS % D_MODEL == 0 and (PREP_STEPS - 1) * D_MODEL == MAIN_COLS + TAIL_ALIGNED

F32 = jnp.float32
BF16 = jnp.bfloat16

def _dot(a, b):
    return jnp.dot(a, b, preferred_element_type=F32)


def _rmsnorm(x, g):
    return x * lax.rsqrt(jnp.mean(x * x, axis=-1, keepdims=True) + EPS) * g


def _gelu_tanh(x):
    c = (2.0 / jnp.pi) ** 0.5
    inner = x * (-2.0 * c - (2.0 * c * 0.044715) * (x * x))
    return x * (1.0 / (1.0 + jnp.exp(inner)))


def _split3(x):
    hi = x.astype(BF16)
    r1 = x - hi.astype(F32)
    mid = r1.astype(BF16)
    lo = (r1 - mid.astype(F32)).astype(BF16)
    return hi, mid, lo


def _prep_kernel(w_in_t_ref, w_o_sgu_ref, w_o_gla_ref, w_out_ref,
                 main_ref, tail_ref, o_sgu_ref, o_gla_ref, out_ref):
    i = pl.program_id(0)
    n_main = MAIN_COLS // D_MODEL
    last = PREP_STEPS - 1

    @pl.when(i < n_main)
    def _():
        main_ref[...] = w_in_t_ref[...].T.astype(BF16)

    @pl.when(jnp.logical_and(i >= n_main, i < last))
    def _():
        tail_ref[...] = w_in_t_ref[...].T.astype(BF16)

    @pl.when(i == last)
    def _():
        rest = IN_COLS - last * D_MODEL
        rows = jnp.concatenate(
            [w_in_t_ref[0:rest, :], jnp.zeros((LANES - rest, D_MODEL), F32)], axis=0)
        tail_ref[:, 0:LANES] = rows.T.astype(BF16)

    o_sgu_ref[...] = pltpu.roll(w_o_sgu_ref[...], GATE_SHIFT, axis=1).astype(BF16)
    o_gla_ref[...] = pltpu.roll(w_o_gla_ref[...], GATE_SHIFT, axis=1).astype(BF16)
    out_ref[...] = pltpu.roll(w_out_ref[...], GATE_SHIFT, axis=0).astype(BF16)


def _mixer_kernel(x_ref, g_ref, w_main_ref, w_tail_ref, w_a2_ref, b_a_ref,
                  sgu_w_ref, sgu_bias_ref, ln_g_ref, ln_b_ref, gnorm_ref,
                  w_o_sgu_ref, w_o_gla_ref, w_out_ref, w_up_src_ref, w_down_src_ref,
                  o_ref, w_up_dst_ref, w_down_dst_ref, state_ref, su_ref, og_ref):
    tb = x_ref.shape[0]
    n_sgu = tb // SGU_CHUNK
    gb = GLA_BLOCK_CHUNKS * GLA_CHUNK
    n_blk = tb // gb
    half = gb // 2

    @pl.when(pl.program_id(1) == 0)
    def _():
        state_ref[...] = jnp.zeros_like(state_ref)

    w_up_dst_ref[...] = w_up_src_ref[...].astype(BF16)
    w_down_dst_ref[...] = w_down_src_ref[...].astype(BF16)

    x = x_ref[...]
    h = _rmsnorm(x, g_ref[...]).astype(BF16)

    def proj(off, width):
        return _dot(h, w_main_ref[:, off:off + width])

    z_u = proj(OFF_ZU, SGU_WIDTH)
    z_v = proj(OFF_ZV, SGU_WIDTH)
    tail = _dot(h, w_tail_ref[...])
    a_low = tail[:, 0:LANES].astype(BF16)
    q = proj(OFF_Q, GLA_KEY_WIDTH) * (GLA_DK ** -0.5)
    k = proj(OFF_K, GLA_KEY_WIDTH)

    u = jax.nn.gelu(z_u)
    v = jax.nn.gelu(z_v)
    mu = jnp.mean(v, axis=-1, keepdims=True)
    vc = v - mu
    var = jnp.mean(vc * vc, axis=-1, keepdims=True)
    vn = (vc * lax.rsqrt(var + EPS) * ln_g_ref[...] + ln_b_ref[...]).astype(BF16)

    vv = proj(OFF_V, GLA_VAL_WIDTH).astype(BF16)
    r = proj(OFF_R, GLA_VAL_WIDTH)
    z = _dot(a_low, w_a2_ref[...]) + b_a_ref[...]
    log_a = jax.nn.log_sigmoid(z) * (1.0 / GLA_GATE_TEMP)

    row = lax.broadcasted_iota(jnp.int32, (SGU_CHUNK, SGU_CHUNK), 0)
    col = lax.broadcasted_iota(jnp.int32, (SGU_CHUNK, SGU_CHUNK), 1)
    causal = row >= col
    for g in range(SGU_GROUPS):
        cols = slice(g * SGU_GROUP_DIM, (g + 1) * SGU_GROUP_DIM)
        wg = jnp.where(causal, sgu_w_ref[g], 0.0).astype(BF16)
        vg = jnp.concatenate(
            [vn[c * SGU_CHUNK:(c + 1) * SGU_CHUNK, cols] for c in range(n_sgu)], axis=1)
        svg = _dot(wg, vg)
        bias = sgu_bias_ref[:, cols]
        for c in range(n_sgu):
            rows = slice(c * SGU_CHUNK, (c + 1) * SGU_CHUNK)
            sv = svg[:, c * SGU_GROUP_DIM:(c + 1) * SGU_GROUP_DIM] + bias
            su_ref[rows, cols] = (u[rows, cols] * sv).astype(BF16)

    ri = lax.broadcasted_iota(jnp.int32, (gb, gb), 0)
    ci = lax.broadcasted_iota(jnp.int32, (gb, gb), 1)
    chunk_gap = ri // GLA_CHUNK - ci // GLA_CHUNK
    intra = jnp.logical_and(chunk_gap == 0, ci <= ri)
    adjacent = chunk_gap == 1
    far = chunk_gap[half:, :half] >= 2
    kw = GLA_KEY_WIDTH
    prefix_sum = jnp.where(intra, 1.0, 0.0).astype(BF16)
    log_a_split = jnp.concatenate(_split3(log_a), axis=1)
    bcums = []
    for blk in range(n_blk):
        parts = _dot(prefix_sum, log_a_split[blk * gb:(blk + 1) * gb])
        bcums.append(parts[:, 0:kw] + parts[:, kw:2 * kw] + parts[:, 2 * kw:3 * kw])

    wrapped = lax.broadcasted_iota(jnp.int32, (tb, LANES), 1) < GATE_SHIFT

    def rotated_gate(base):
        first = jnp.where(wrapped, tail[:, base + D_MODEL:base + D_MODEL + LANES],
                          tail[:, base:base + LANES])
        return jnp.concatenate([first, tail[:, base + LANES:base + D_MODEL]], axis=1)

    gate_sgu_pre = rotated_gate(0)
    gate_gla_pre = rotated_gate(D_MODEL)
    y_sgu = _dot(su_ref[...], w_o_sgu_ref[...])

    def per_chunk_rows(rows):
        return jnp.concatenate([jnp.broadcast_to(r, (GLA_CHUNK, kw)) for r in rows], axis=0)

    def dot_nt(a, b):
        return lax.dot_general(a, b, (((1,), (1,)), ((), ())), preferred_element_type=F32)

    one = jnp.ones((1, kw), F32)
    for blk in range(n_blk):
        rows = slice(blk * gb, (blk + 1) * gb)
        bcum = bcums[blk]
        q_b = q[rows]
        k_b = k[rows]
        bl = [bcum[(c + 1) * GLA_CHUNK - 1:(c + 1) * GLA_CHUNK, :] for c in range(GLA_BLOCK_CHUNKS)]
        c2 = bl[0] + bl[1]
        c3 = c2 + bl[2]
        c4 = c3 + bl[3]

        q_e32 = q_b * jnp.exp(bcum)
        k_state32 = k_b * jnp.exp(per_chunk_rows(bl) - bcum)
        q_e = q_e32.astype(BF16)
        k_intra = (k_b * jnp.exp(-bcum)).astype(BF16)
        k_state = k_state32.astype(BF16)
        q_blk = (q_e32 * per_chunk_rows(
            [one, jnp.exp(bl[0]), jnp.exp(c2), jnp.exp(c3)])).astype(BF16)
        k_blk = (k_state32 * per_chunk_rows(
            [jnp.exp(bl[1] + bl[2] + bl[3]), jnp.exp(bl[2] + bl[3]), jnp.exp(bl[3]), one])
                 ).astype(BF16)
        q_far = (q_e32[half:] * per_chunk_rows([one, jnp.exp(bl[2])])).astype(BF16)
        k_far = (k_state32[:half] * per_chunk_rows([jnp.exp(bl[1]), one])).astype(BF16)
        block_decay = jnp.exp(
            jnp.concatenate([c4, jnp.zeros((SUBLANES - 1, kw), F32)], axis=0))

        for hd in range(GLA_HEADS):
            kc = slice(hd * GLA_DK, (hd + 1) * GLA_DK)
            vcols = slice(hd * GLA_DV, (hd + 1) * GLA_DV)
            att = jnp.where(intra, dot_nt(q_e[:, kc], k_intra[:, kc]),
                            jnp.where(adjacent, dot_nt(q_e[:, kc], k_state[:, kc]), 0.0))
            att_far = jnp.where(far, dot_nt(q_far[:, kc], k_far[:, kc]), 0.0)
            att = jnp.concatenate(
                [att[:half],
                 jnp.concatenate([att[half:, :half] + att_far, att[half:, half:]], axis=1)],
                axis=0).astype(BF16)
            v_h = vv[rows, vcols]
            state = state_ref[hd]
            o = _dot(att, v_h) + _dot(q_blk[:, kc], state.astype(BF16))
            kv = lax.dot_general(k_blk[:, kc], v_h, (((0,), (0,)), ((), ())),
                                 preferred_element_type=F32)
            state_ref[hd] = state * block_decay[:, kc].T[:, 0:1] + kv
            o = _rmsnorm(o, gnorm_ref[...])
            og_ref[rows, vcols] = (o * jax.nn.silu(r[rows, vcols])).astype(BF16)
    y_gla = _dot(og_ref[...], w_o_gla_ref[...])

    gate_sgu = jax.nn.sigmoid(gate_sgu_pre)
    gate_gla = jax.nn.sigmoid(gate_gla_pre)
    merged = (gate_sgu * y_sgu + gate_gla * y_gla).astype(BF16)
    o_ref[...] = x + _dot(merged, w_out_ref[...])


def _ffn_kernel(x_ref, g_ref, w_up_ref, conv_w_ref, conv_b_ref, w_down_ref, final_g_ref,
                o_ref, hbuf_ref, *, final_norm):
    tb = x_ref.shape[0]
    pad = SUBLANES
    f = FFN_HIDDEN

    @pl.when(pl.program_id(1) == 0)
    def _():
        hbuf_ref[0:pad, :] = jnp.zeros((pad, 2 * f), F32)

    x = x_ref[...]
    hn = _rmsnorm(x, g_ref[...]).astype(BF16)

    hbuf_ref[pad:pad + tb, :] = _dot(hn, w_up_ref[...])
    hc = conv_b_ref[...]
    for j in range(CONV_WIDTH):
        shift = CONV_WIDTH - 1 - j
        hc = hc + hbuf_ref[pad - shift:pad - shift + tb, :] * conv_w_ref[j:j + 1, :]
    hbuf_ref[0:pad, :] = hbuf_ref[tb:tb + pad, :]
    a = hc[:, 0:f]
    b = hc[:, f:2 * f]
    y = x + _dot((jax.nn.silu(a) * b).astype(BF16), w_down_ref[...])
    if final_norm:
        y = _rmsnorm(y, final_g_ref[...])
    o_ref[...] = y


def _resident(shape):
    zeros = (0,) * len(shape)
    return pl.BlockSpec(shape, lambda b, s: zeros, pipeline_mode=pl.Buffered(1))


def _token_block(tb):
    return pl.BlockSpec((None, tb, D_MODEL), lambda b, s: (b, s, 0))


def _mixer_call(x, weights, layer, w_up, w_down):
    batch, seq, _ = x.shape
    tb = MIXER_BLOCK
    steps = seq // tb
    assert seq % tb == 0 and tb % SGU_CHUNK == 0 and tb % (GLA_BLOCK_CHUNKS * GLA_CHUNK) == 0

    def cast_slices(w):
        rows, cols = w.shape[1:]
        assert rows % (steps * BF16_ROWS) == 0

        def slice_of(b, s):
            return jnp.minimum(b * steps + s, steps - 1)

        return (pl.BlockSpec((None, rows // steps, cols), lambda b, s: (layer, slice_of(b, s), 0)),
                pl.BlockSpec((rows // steps, cols), lambda b, s: (slice_of(b, s), 0)),
                jax.ShapeDtypeStruct((rows, cols), BF16))

    up_in, up_out, up_shape = cast_slices(w_up)
    down_in, down_out, down_shape = cast_slices(w_down)
    return pl.pallas_call(
        _mixer_kernel,
        grid=(batch, steps),
        in_specs=[_token_block(tb)] + [_resident(w.shape) for w in weights] + [up_in, down_in],
        out_specs=[_token_block(tb), up_out, down_out],
        out_shape=[jax.ShapeDtypeStruct(x.shape, x.dtype), up_shape, down_shape],
        scratch_shapes=[
            pltpu.VMEM((GLA_HEADS, GLA_DK, GLA_DV), F32),
            pltpu.VMEM((tb, SGU_WIDTH), BF16),
            pltpu.VMEM((tb, GLA_VAL_WIDTH), BF16),
        ],
        compiler_params=pltpu.CompilerParams(
            dimension_semantics=("arbitrary", "arbitrary"),
            vmem_limit_bytes=VMEM_LIMIT_BYTES),
        name="mixer",
    )(x, *weights, w_up, w_down)


def _ffn_call(x, weights, final_norm):
    batch, seq, _ = x.shape
    tb = FFN_BLOCK
    assert seq % tb == 0
    return pl.pallas_call(
        functools.partial(_ffn_kernel, final_norm=final_norm),
        grid=(batch, seq // tb),
        in_specs=[_token_block(tb)] + [_resident(w.shape) for w in weights],
        out_specs=_token_block(tb),
        out_shape=jax.ShapeDtypeStruct(x.shape, x.dtype),
        scratch_shapes=[pltpu.VMEM((tb + SUBLANES, 2 * FFN_HIDDEN), F32)],
        compiler_params=pltpu.CompilerParams(
            dimension_semantics=("arbitrary", "arbitrary"),
            vmem_limit_bytes=VMEM_LIMIT_BYTES),
        name="ffn_final" if final_norm else "ffn",
    )(x, *weights)


def _prep_call(layer, w_in_t, w_o_sgu, w_o_gla, w_out):
    n = PREP_STEPS

    def row_blocks(arr):
        rows, cols = arr.shape[1:]
        return (pl.BlockSpec((None, rows // n, cols), lambda i: (layer, i, 0)),
                pl.BlockSpec((rows // n, cols), lambda i: (i, 0)))

    ins, outs, shapes = [], [], []

    def add(in_spec, out_spec, shape):
        ins.append(in_spec)
        outs.append(out_spec)
        shapes.append(jax.ShapeDtypeStruct(shape, BF16))

    d = D_MODEL
    n_main = MAIN_COLS // d
    ins.append(pl.BlockSpec((None, d, d), lambda i: (layer, i, 0)))
    outs.append(pl.BlockSpec((d, d), lambda i: (0, jnp.minimum(i, n_main - 1))))
    shapes.append(jax.ShapeDtypeStruct((d, MAIN_COLS), BF16))
    outs.append(pl.BlockSpec((d, d), lambda i: (0, jnp.maximum(i - n_main, 0))))
    shapes.append(jax.ShapeDtypeStruct((d, TAIL_COLS), BF16))
    add(*row_blocks(w_o_sgu), w_o_sgu.shape[1:])
    add(*row_blocks(w_o_gla), w_o_gla.shape[1:])
    add(pl.BlockSpec((None, d, d // n), lambda i: (layer, 0, i)),
        pl.BlockSpec((d, d // n), lambda i: (0, i)), w_out.shape[1:])
    return pl.pallas_call(
        _prep_kernel,
        grid=(n,),
        in_specs=ins,
        out_specs=outs,
        out_shape=shapes,
        compiler_params=pltpu.CompilerParams(
            dimension_semantics=("arbitrary",),
            vmem_limit_bytes=VMEM_LIMIT_BYTES),
        name="prep",
    )(w_in_t, w_o_sgu, w_o_gla, w_out)


def _row(v):
    return v.reshape(1, -1).astype(F32)


def kernel(x, norm_mix_g, w_in, w_a2, b_a, sgu_w, sgu_b, sgu_ln_g, sgu_ln_b, gla_norm_g,
           w_o_sgu, w_o_gla, w_out, norm_ffn_g, w_up, conv_w, conv_b, w_down, final_norm_g):
    depth = w_in.shape[0]
    assert w_in.shape[2] == IN_COLS
    w_in_t = jnp.swapaxes(w_in, 1, 2)
    for l in range(depth):
        w_main, w_tail, w_o_sgu_l, w_o_gla_l, w_out_l = _prep_call(
            l, w_in_t, w_o_sgu, w_o_gla, w_out)
        w_a2p = jnp.pad(w_a2[l], ((0, LANES - GLA_GATE_RANK), (0, 0)))
        sgu_bias = jnp.repeat(sgu_b[l].T, SGU_GROUP_DIM, axis=1)
        mixer_weights = (
            _row(norm_mix_g[l]),
            w_main,
            w_tail,
            w_a2p.astype(BF16),
            _row(b_a[l]),
            sgu_w[l],
            sgu_bias,
            _row(sgu_ln_g[l]),
            _row(sgu_ln_b[l]),
            _row(gla_norm_g[l]),
            w_o_sgu_l,
            w_o_gla_l,
            w_out_l,
        )
        x, w_up_l, w_down_l = _mixer_call(x, mixer_weights, l, w_up, w_down)
        ffn_weights = (
            _row(norm_ffn_g[l]),
            w_up_l,
            conv_w[l],
            _row(conv_b[l]),
            w_down_l,
            _row(final_norm_g),
        )
        x = _ffn_call(x, ffn_weights, final_norm=(l == depth - 1))
    return x
```

```python
import functools

import jax
import jax.numpy as jnp
from jax import lax
from jax.experimental import pallas as pl
from jax.experimental.pallas import tpu as pltpu

D_MODEL = 1024
SGU_CHUNK = 128
SGU_GROUPS = 8
SGU_GROUP_DIM = 128
SGU_WIDTH = SGU_GROUPS * SGU_GROUP_DIM
GLA_HEADS = 4
GLA_DK = 128
GLA_DV = 256
GLA_KEY_WIDTH = GLA_HEADS * GLA_DK
GLA_VAL_WIDTH = GLA_HEADS * GLA_DV
GLA_GATE_RANK = 16
GLA_GATE_TEMP = 16.0
GLA_CHUNK = 64
FFN_HIDDEN = 2816
CONV_WIDTH = 3
EPS = 1e-6

LANES = 128
SUBLANES = 8
BF16_ROWS = 16
VMEM_LIMIT_BYTES = 56 * 1024 * 1024

OFF_ZU = 0
OFF_ZV = OFF_ZU + SGU_WIDTH
OFF_Q = OFF_ZV + SGU_WIDTH
OFF_K = OFF_Q + GLA_KEY_WIDTH
OFF_V = OFF_K + GLA_KEY_WIDTH
OFF_R = OFF_V + GLA_VAL_WIDTH
MAIN_COLS = OFF_R + GLA_VAL_WIDTH
IN_COLS = MAIN_COLS + GLA_GATE_RANK + 2 * D_MODEL
TAIL_ALIGNED = (IN_COLS - MAIN_COLS) // LANES * LANES
TAIL_COLS = TAIL_ALIGNED + LANES
GATE_SHIFT = GLA_GATE_RANK

MIXER_BLOCK = 512
FFN_BLOCK = 512
GLA_BLOCK_CHUNKS = 4
WEIGHT_STAGE_ROWS = 256

F32 = jnp.float32
BF16 = jnp.bfloat16

def _dot(a, b):
    return jnp.dot(a, b, preferred_element_type=F32)


def _rmsnorm(x, g):
    return x * lax.rsqrt(jnp.mean(x * x, axis=-1, keepdims=True) + EPS) * g


def _split3(x):
    hi = x.astype(BF16)
    r1 = x - hi.astype(F32)
    mid = r1.astype(BF16)
    lo = (r1 - mid.astype(F32)).astype(BF16)
    return hi, mid, lo


def _load_mixer_weights(layer, w_in_t_hbm, w_o_sgu_hbm, w_o_gla_hbm, w_out_hbm,
                        w_main_ref, w_tail_ref, w_o_sgu_ref, w_o_gla_ref, w_out_ref,
                        stage_ref, sems):
    r = stage_ref.shape[1]
    d = D_MODEL
    jobs = []

    def transposed_into(col):
        def convert(slot_ref):
            block = slot_ref[...].T.astype(BF16)
            if col < MAIN_COLS:
                w_main_ref[:, col:col + r] = block
            else:
                w_tail_ref[:, col - MAIN_COLS:col - MAIN_COLS + r] = block
        return convert

    n_full = IN_COLS // r
    rest = IN_COLS - n_full * r
    assert MAIN_COLS % r == 0 and n_full * r == MAIN_COLS + TAIL_ALIGNED and rest <= LANES
    for j in range(n_full):
        jobs.append(([(w_in_t_hbm.at[layer, pl.ds(j * r, r), :], 0, r)],
                     transposed_into(j * r)))

    def tail_end(slot_ref):
        rows = jnp.concatenate(
            [slot_ref[0:rest, :], jnp.zeros((LANES - rest, d), F32)], axis=0)
        w_tail_ref[:, TAIL_ALIGNED:TAIL_COLS] = rows.T.astype(BF16)

    jobs.append(([(w_in_t_hbm.at[layer, pl.ds(n_full * r, rest), :], 0, rest)], tail_end))

    def rotated_rows_into(dst_ref, row):
        def convert(slot_ref):
            dst_ref[row:row + r, :] = pltpu.roll(slot_ref[...], GATE_SHIFT, axis=1).astype(BF16)
        return convert

    for src, dst in ((w_o_sgu_hbm, w_o_sgu_ref), (w_o_gla_hbm, w_o_gla_ref)):
        for row in range(0, d, r):
            jobs.append(([(src.at[layer, pl.ds(row, r), :], 0, r)],
                         rotated_rows_into(dst, row)))

    def rows_into(row):
        def convert(slot_ref):
            w_out_ref[row:row + r, :] = slot_ref[...].astype(BF16)
        return convert

    s = GATE_SHIFT
    jobs.append(([(w_out_hbm.at[layer, pl.ds(d - s, s), :], 0, s),
                  (w_out_hbm.at[layer, pl.ds(0, r - s), :], s, r - s)], rows_into(0)))
    for row in range(r, d, r):
        jobs.append(([(w_out_hbm.at[layer, pl.ds(row - s, r), :], 0, r)], rows_into(row)))

    def copies(i):
        slot = i % 2
        return [pltpu.make_async_copy(src, stage_ref.at[slot, pl.ds(off, n), :],
                                      sems.at[slot, p])
                for p, (src, off, n) in enumerate(jobs[i][0])]

    for c in copies(0):
        c.start()
    for i, (_, convert) in enumerate(jobs):
        if i + 1 < len(jobs):
            for c in copies(i + 1):
                c.start()
        for c in copies(i):
            c.wait()
        convert(stage_ref.at[i % 2])


def _mixer_kernel(x_ref, g_ref, w_a2_ref, b_a_ref,
                  sgu_w_ref, sgu_bias_ref, ln_g_ref, ln_b_ref, gnorm_ref,
                  w_up_src_ref, w_down_src_ref,
                  w_in_t_hbm, w_o_sgu_hbm, w_o_gla_hbm, w_out_hbm,
                  o_ref, w_up_dst_ref, w_down_dst_ref, state_ref, su_ref, og_ref,
                  w_main_ref, w_tail_ref, w_o_sgu_ref, w_o_gla_ref, w_out_ref,
                  stage_ref, sems, *, layer):
    tb = x_ref.shape[0]
    n_sgu = tb // SGU_CHUNK
    gb = GLA_BLOCK_CHUNKS * GLA_CHUNK
    n_blk = tb // gb
    half = gb // 2

    @pl.when(jnp.logical_and(pl.program_id(0) == 0, pl.program_id(1) == 0))
    def _():
        _load_mixer_weights(layer, w_in_t_hbm, w_o_sgu_hbm, w_o_gla_hbm, w_out_hbm,
                            w_main_ref, w_tail_ref, w_o_sgu_ref, w_o_gla_ref, w_out_ref,
                            stage_ref, sems)

    @pl.when(pl.program_id(1) == 0)
    def _():
        state_ref[...] = jnp.zeros_like(state_ref)

    w_up_dst_ref[...] = w_up_src_ref[...].astype(BF16)
    w_down_dst_ref[...] = w_down_src_ref[...].astype(BF16)

    x = x_ref[...]
    h = _rmsnorm(x, g_ref[...]).astype(BF16)

    def proj(off, width):
        return _dot(h, w_main_ref[:, off:off + width])

    kw = GLA_KEY_WIDTH
    t = {}

    def project_u():
        t["z_u"] = proj(OFF_ZU, SGU_WIDTH)

    def project_v():
        t["z_v"] = proj(OFF_ZV, SGU_WIDTH)

    def gelu_u():
        t["u"] = jax.nn.gelu(t["z_u"])

    def project_tail():
        t["tail"] = _dot(h, w_tail_ref[...])
        t["a_low"] = t["tail"][:, 0:LANES].astype(BF16)

    def norm_v():
        v = jax.nn.gelu(t["z_v"])
        mu = jnp.mean(v, axis=-1, keepdims=True)
        vc = v - mu
        var = jnp.mean(vc * vc, axis=-1, keepdims=True)
        t["vn"] = (vc * lax.rsqrt(var + EPS) * ln_g_ref[...] + ln_b_ref[...]).astype(BF16)

    def log_decay():
        z = _dot(t["a_low"], w_a2_ref[...]) + b_a_ref[...]
        t["log_a"] = jax.nn.log_sigmoid(z) * (1.0 / GLA_GATE_TEMP)

    ri = lax.broadcasted_iota(jnp.int32, (gb, gb), 0)
    ci = lax.broadcasted_iota(jnp.int32, (gb, gb), 1)
    chunk_gap = ri // GLA_CHUNK - ci // GLA_CHUNK
    intra = jnp.logical_and(chunk_gap == 0, ci <= ri)
    adjacent = chunk_gap == 1
    far = chunk_gap[half:, :half] >= 2

    def prefix_sums():
        prefix_sum = jnp.where(intra, 1.0, 0.0).astype(BF16)
        log_a_split = jnp.concatenate(_split3(t["log_a"]), axis=1)
        bcums = []
        for blk in range(n_blk):
            parts = _dot(prefix_sum, log_a_split[blk * gb:(blk + 1) * gb])
            bcums.append(parts[:, 0:kw] + parts[:, kw:2 * kw] + parts[:, 2 * kw:3 * kw])
        t["bcums"] = bcums

    def project_qk():
        t["q"] = proj(OFF_Q, GLA_KEY_WIDTH) * (GLA_DK ** -0.5)
        t["k"] = proj(OFF_K, GLA_KEY_WIDTH)

    def sgu():
        row = lax.broadcasted_iota(jnp.int32, (SGU_CHUNK, SGU_CHUNK), 0)
        col = lax.broadcasted_iota(jnp.int32, (SGU_CHUNK, SGU_CHUNK), 1)
        causal = row >= col
        vn, u = t["vn"], t["u"]
        for g in range(SGU_GROUPS):
            cols = slice(g * SGU_GROUP_DIM, (g + 1) * SGU_GROUP_DIM)
            wg = jnp.where(causal, sgu_w_ref[g], 0.0).astype(BF16)
            vg = jnp.concatenate(
                [vn[c * SGU_CHUNK:(c + 1) * SGU_CHUNK, cols] for c in range(n_sgu)], axis=1)
            svg = _dot(wg, vg)
            bias = sgu_bias_ref[:, cols]
            for c in range(n_sgu):
                rows = slice(c * SGU_CHUNK, (c + 1) * SGU_CHUNK)
                sv = svg[:, c * SGU_GROUP_DIM:(c + 1) * SGU_GROUP_DIM] + bias
                su_ref[rows, cols] = (u[rows, cols] * sv).astype(BF16)

    def project_vv():
        t["vv"] = proj(OFF_V, GLA_VAL_WIDTH).astype(BF16)

    def project_r():
        t["r"] = proj(OFF_R, GLA_VAL_WIDTH)

    def gates():
        tail = t["tail"]
        wrapped = lax.broadcasted_iota(jnp.int32, (tb, LANES), 1) < GATE_SHIFT

        def rotated_gate(base):
            first = jnp.where(wrapped, tail[:, base + D_MODEL:base + D_MODEL + LANES],
                              tail[:, base:base + LANES])
            return jnp.concatenate([first, tail[:, base + LANES:base + D_MODEL]], axis=1)

        t["gate_sgu"] = jax.nn.sigmoid(rotated_gate(0))
        t["gate_gla"] = jax.nn.sigmoid(rotated_gate(D_MODEL))

    def project_sgu_out():
        t["y_sgu"] = _dot(su_ref[...], w_o_sgu_ref[...])

    def per_chunk_rows(rows):
        return jnp.concatenate([jnp.broadcast_to(r, (GLA_CHUNK, kw)) for r in rows], axis=0)

    def dot_nt(a, b):
        return lax.dot_general(a, b, (((1,), (1,)), ((), ())), preferred_element_type=F32)

    def gla_block(blk):
        one = jnp.ones((1, kw), F32)
        vv, r = t["vv"], t["r"]
        rows = slice(blk * gb, (blk + 1) * gb)
        bcum = t["bcums"][blk]
        q_b = t["q"][rows]
        k_b = t["k"][rows]
        bl = [bcum[(c + 1) * GLA_CHUNK - 1:(c + 1) * GLA_CHUNK, :] for c in range(GLA_BLOCK_CHUNKS)]
        c2 = bl[0] + bl[1]
        c3 = c2 + bl[2]
        c4 = c3 + bl[3]

        q_e32 = q_b * jnp.exp(bcum)
        k_state32 = k_b * jnp.exp(per_chunk_rows(bl) - bcum)
        q_e = q_e32.astype(BF16)
        k_intra = (k_b * jnp.exp(-bcum)).astype(BF16)
        k_state = k_state32.astype(BF16)
        q_blk = (q_e32 * per_chunk_rows(
            [one, jnp.exp(bl[0]), jnp.exp(c2), jnp.exp(c3)])).astype(BF16)
        k_blk = (k_state32 * per_chunk_rows(
            [jnp.exp(bl[1] + bl[2] + bl[3]), jnp.exp(bl[2] + bl[3]), jnp.exp(bl[3]), one])
                 ).astype(BF16)
        q_far = (q_e32[half:] * per_chunk_rows([one, jnp.exp(bl[2])])).astype(BF16)
        k_far = (k_state32[:half] * per_chunk_rows([jnp.exp(bl[1]), one])).astype(BF16)
        block_decay = jnp.exp(
            jnp.concatenate([c4, jnp.zeros((SUBLANES - 1, kw), F32)], axis=0))

        for hd in range(GLA_HEADS):
            kc = slice(hd * GLA_DK, (hd + 1) * GLA_DK)
            vcols = slice(hd * GLA_DV, (hd + 1) * GLA_DV)
            att = jnp.where(intra, dot_nt(q_e[:, kc], k_intra[:, kc]),
                            jnp.where(adjacent, dot_nt(q_e[:, kc], k_state[:, kc]), 0.0))
            att_far = jnp.where(far, dot_nt(q_far[:, kc], k_far[:, kc]), 0.0)
            att = jnp.concatenate(
                [att[:half],
                 jnp.concatenate([att[half:, :half] + att_far, att[half:, half:]], axis=1)],
                axis=0).astype(BF16)
            v_h = vv[rows, vcols]
            state = state_ref[hd]
            o = _dot(att, v_h) + _dot(q_blk[:, kc], state.astype(BF16))
            kv = lax.dot_general(k_blk[:, kc], v_h, (((0,), (0,)), ((), ())),
                                 preferred_element_type=F32)
            state_ref[hd] = state * block_decay[:, kc].T[:, 0:1] + kv
            o = _rmsnorm(o, gnorm_ref[...])
            og_ref[rows, vcols] = (o * jax.nn.silu(r[rows, vcols])).astype(BF16)

    gla_blocks = [functools.partial(gla_block, blk) for blk in range(n_blk)]

    def project_gla_out():
        t["y_gla"] = _dot(og_ref[...], w_o_gla_ref[...])

    def merge_and_project():
        merged = (t["gate_sgu"] * t["y_sgu"] + t["gate_gla"] * t["y_gla"]).astype(BF16)
        o_ref[...] = x + _dot(merged, w_out_ref[...])

    assert n_blk == 2
    for stage in (project_u, gelu_u, project_v, norm_v, project_tail, project_qk, log_decay,
                  project_vv, project_r, sgu, prefix_sums, gates, project_sgu_out,
                  gla_blocks[0], gla_blocks[1], project_gla_out, merge_and_project):
        stage()


def _ffn_kernel(x_ref, g_ref, w_up_ref, conv_w_ref, conv_b_ref, w_down_ref, final_g_ref,
                o_ref, hbuf_ref, *, final_norm):
    tb = x_ref.shape[0]
    pad = SUBLANES
    f = FFN_HIDDEN

    @pl.when(pl.program_id(1) == 0)
    def _():
        hbuf_ref[0:pad, :] = jnp.zeros((pad, 2 * f), F32)

    x = x_ref[...]
    hn = _rmsnorm(x, g_ref[...]).astype(BF16)

    hbuf_ref[pad:pad + tb, :] = _dot(hn, w_up_ref[...])
    hc = conv_b_ref[...]
    for j in range(CONV_WIDTH):
        shift = CONV_WIDTH - 1 - j
        hc = hc + hbuf_ref[pad - shift:pad - shift + tb, :] * conv_w_ref[j:j + 1, :]
    hbuf_ref[0:pad, :] = hbuf_ref[tb:tb + pad, :]
    a = hc[:, 0:f]
    b = hc[:, f:2 * f]
    y = x + _dot((jax.nn.silu(a) * b).astype(BF16), w_down_ref[...])
    if final_norm:
        y = _rmsnorm(y, final_g_ref[...])
    o_ref[...] = y


def _resident(shape):
    zeros = (0,) * len(shape)
    return pl.BlockSpec(shape, lambda b, s: zeros, pipeline_mode=pl.Buffered(1))


def _token_block(tb):
    return pl.BlockSpec((None, tb, D_MODEL), lambda b, s: (b, s, 0))


def _mixer_call(x, weights, layer, w_up, w_down, w_in_t, w_o_sgu, w_o_gla, w_out):
    batch, seq, _ = x.shape
    tb = MIXER_BLOCK
    steps = seq // tb
    assert seq % tb == 0 and tb % SGU_CHUNK == 0 and tb % (GLA_BLOCK_CHUNKS * GLA_CHUNK) == 0

    def cast_slices(w):
        rows, cols = w.shape[1:]
        assert rows % (steps * BF16_ROWS) == 0

        def slice_of(b, s):
            return jnp.minimum(b * steps + s, steps - 1)

        return (pl.BlockSpec((None, rows // steps, cols), lambda b, s: (layer, slice_of(b, s), 0)),
                pl.BlockSpec((rows // steps, cols), lambda b, s: (slice_of(b, s), 0)),
                jax.ShapeDtypeStruct((rows, cols), BF16))

    up_in, up_out, up_shape = cast_slices(w_up)
    down_in, down_out, down_shape = cast_slices(w_down)
    in_hbm = pl.BlockSpec(memory_space=pl.ANY)
    d = D_MODEL
    return pl.pallas_call(
        functools.partial(_mixer_kernel, layer=layer),
        grid=(batch, steps),
        in_specs=[_token_block(tb)] + [_resident(w.shape) for w in weights]
        + [up_in, down_in] + [in_hbm] * 4,
        out_specs=[_token_block(tb), up_out, down_out],
        out_shape=[jax.ShapeDtypeStruct(x.shape, x.dtype), up_shape, down_shape],
        scratch_shapes=[
            pltpu.VMEM((GLA_HEADS, GLA_DK, GLA_DV), F32),
            pltpu.VMEM((tb, SGU_WIDTH), BF16),
            pltpu.VMEM((tb, GLA_VAL_WIDTH), BF16),
            pltpu.VMEM((d, MAIN_COLS), BF16),
            pltpu.VMEM((d, TAIL_COLS), BF16),
            pltpu.VMEM((d, d), BF16),
            pltpu.VMEM((d, d), BF16),
            pltpu.VMEM((d, d), BF16),
            pltpu.VMEM((2, WEIGHT_STAGE_ROWS, d), F32),
            pltpu.SemaphoreType.DMA((2, 2)),
        ],
        compiler_params=pltpu.CompilerParams(
            dimension_semantics=("arbitrary", "arbitrary"),
            vmem_limit_bytes=VMEM_LIMIT_BYTES),
        name="mixer",
    )(x, *weights, w_up, w_down, w_in_t, w_o_sgu, w_o_gla, w_out)


def _ffn_call(x, weights, final_norm):
    batch, seq, _ = x.shape
    tb = FFN_BLOCK
    assert seq % tb == 0
    return pl.pallas_call(
        functools.partial(_ffn_kernel, final_norm=final_norm),
        grid=(batch, seq // tb),
        in_specs=[_token_block(tb)] + [_resident(w.shape) for w in weights],
        out_specs=_token_block(tb),
        out_shape=jax.ShapeDtypeStruct(x.shape, x.dtype),
        scratch_shapes=[pltpu.VMEM((tb + SUBLANES, 2 * FFN_HIDDEN), F32)],
        compiler_params=pltpu.CompilerParams(
            dimension_semantics=("arbitrary", "arbitrary"),
            vmem_limit_bytes=VMEM_LIMIT_BYTES),
        name="ffn_final" if final_norm else "ffn",
    )(x, *weights)


def _row(v):
    return v.reshape(1, -1).astype(F32)


def kernel(x, norm_mix_g, w_in, w_a2, b_a, sgu_w, sgu_b, sgu_ln_g, sgu_ln_b, gla_norm_g,
           w_o_sgu, w_o_gla, w_out, norm_ffn_g, w_up, conv_w, conv_b, w_down, final_norm_g):
    depth = w_in.shape[0]
    assert w_in.shape[2] == IN_COLS
    w_in_t = jnp.swapaxes(w_in, 1, 2)
    for l in range(depth):
        w_a2p = jnp.pad(w_a2[l], ((0, LANES - GLA_GATE_RANK), (0, 0)))
        sgu_bias = jnp.repeat(sgu_b[l].T, SGU_GROUP_DIM, axis=1)
        mixer_weights = (
            _row(norm_mix_g[l]),
            w_a2p.astype(BF16),
            _row(b_a[l]),
            sgu_w[l],
            sgu_bias,
            _row(sgu_ln_g[l]),
            _row(sgu_ln_b[l]),
            _row(gla_norm_g[l]),
        )
        x, w_up_l, w_down_l = _mixer_call(x, mixer_weights, l, w_up, w_down,
                                          w_in_t, w_o_sgu, w_o_gla, w_out)
        ffn_weights = (
            _row(norm_ffn_g[l]),
            w_up_l,
            conv_w[l],
            _row(conv_b[l]),
            w_down_l,
            _row(final_norm_g),
        )
        x = _ffn_call(x, ffn_weights, final_norm=(l == depth - 1))
    return x
```

```python
import functools

import jax
import jax.numpy as jnp
from jax import lax
from jax.experimental import pallas as pl
from jax.experimental.pallas import tpu as pltpu

D_MODEL = 1024
SGU_CHUNK = 128
SGU_GROUPS = 8
SGU_GROUP_DIM = 128
SGU_WIDTH = SGU_GROUPS * SGU_GROUP_DIM
GLA_HEADS = 4
GLA_DK = 128
GLA_DV = 256
GLA_KEY_WIDTH = GLA_HEADS * GLA_DK
GLA_VAL_WIDTH = GLA_HEADS * GLA_DV
GLA_GATE_RANK = 16
GLA_GATE_TEMP = 16.0
GLA_CHUNK = 64
FFN_HIDDEN = 2816
CONV_WIDTH = 3
EPS = 1e-6

LANES = 128
SUBLANES = 8
BF16_ROWS = 16
VMEM_LIMIT_BYTES = 56 * 1024 * 1024

OFF_ZU = 0
OFF_ZV = OFF_ZU + SGU_WIDTH
OFF_Q = OFF_ZV + SGU_WIDTH
OFF_K = OFF_Q + GLA_KEY_WIDTH
OFF_V = OFF_K + GLA_KEY_WIDTH
OFF_R = OFF_V + GLA_VAL_WIDTH
MAIN_COLS = OFF_R + GLA_VAL_WIDTH
IN_COLS = MAIN_COLS + GLA_GATE_RANK + 2 * D_MODEL
TAIL_ALIGNED = (IN_COLS - MAIN_COLS) // LANES * LANES
TAIL_COLS = TAIL_ALIGNED + LANES
GATE_SHIFT = GLA_GATE_RANK

MIXER_BLOCK = 512
FFN_BLOCK = 512
GLA_BLOCK_CHUNKS = 4
WEIGHT_STAGE_ROWS = 128
WEIGHT_STAGE_SLOTS = 4

F32 = jnp.float32
BF16 = jnp.bfloat16

def _dot(a, b):
    return jnp.dot(a, b, preferred_element_type=F32)


def _rmsnorm(x, g):
    return x * lax.rsqrt(jnp.mean(x * x, axis=-1, keepdims=True) + EPS) * g


def _split3(x):
    hi = x.astype(BF16)
    r1 = x - hi.astype(F32)
    mid = r1.astype(BF16)
    lo = (r1 - mid.astype(F32)).astype(BF16)
    return hi, mid, lo


def _load_mixer_weights(layer, w_in_t_hbm, w_o_sgu_hbm, w_o_gla_hbm, w_out_hbm,
                        w_main_ref, w_tail_ref, w_o_sgu_ref, w_o_gla_ref, w_out_ref,
                        stage_ref, sems):
    r = stage_ref.shape[1]
    d = D_MODEL
    jobs = []

    def transposed_into(col):
        def convert(slot_ref):
            block = slot_ref[...].T.astype(BF16)
            if col < MAIN_COLS:
                w_main_ref[:, col:col + r] = block
            else:
                w_tail_ref[:, col - MAIN_COLS:col - MAIN_COLS + r] = block
        return convert

    n_full = IN_COLS // r
    rest = IN_COLS - n_full * r
    assert MAIN_COLS % r == 0 and n_full * r == MAIN_COLS + TAIL_ALIGNED and rest <= LANES
    for j in range(n_full):
        jobs.append(([(w_in_t_hbm.at[layer, pl.ds(j * r, r), :], 0, r)],
                     transposed_into(j * r)))

    def tail_end(slot_ref):
        rows = jnp.concatenate(
            [slot_ref[0:rest, :], jnp.zeros((LANES - rest, d), F32)], axis=0)
        w_tail_ref[:, TAIL_ALIGNED:TAIL_COLS] = rows.T.astype(BF16)

    jobs.append(([(w_in_t_hbm.at[layer, pl.ds(n_full * r, rest), :], 0, rest)], tail_end))

    def rotated_rows_into(dst_ref, row):
        def convert(slot_ref):
            dst_ref[row:row + r, :] = pltpu.roll(slot_ref[...], GATE_SHIFT, axis=1).astype(BF16)
        return convert

    for src, dst in ((w_o_sgu_hbm, w_o_sgu_ref), (w_o_gla_hbm, w_o_gla_ref)):
        for row in range(0, d, r):
            jobs.append(([(src.at[layer, pl.ds(row, r), :], 0, r)],
                         rotated_rows_into(dst, row)))

    def rows_into(row):
        def convert(slot_ref):
            w_out_ref[row:row + r, :] = slot_ref[...].astype(BF16)
        return convert

    s = GATE_SHIFT
    jobs.append(([(w_out_hbm.at[layer, pl.ds(d - s, s), :], 0, s),
                  (w_out_hbm.at[layer, pl.ds(0, r - s), :], s, r - s)], rows_into(0)))
    for row in range(r, d, r):
        jobs.append(([(w_out_hbm.at[layer, pl.ds(row - s, r), :], 0, r)], rows_into(row)))

    n_slots = stage_ref.shape[0]

    def copies(i):
        slot = i % n_slots
        return [pltpu.make_async_copy(src, stage_ref.at[slot, pl.ds(off, n), :],
                                      sems.at[slot, p])
                for p, (src, off, n) in enumerate(jobs[i][0])]

    for i in range(min(n_slots - 1, len(jobs))):
        for c in copies(i):
            c.start()
    for i, (_, convert) in enumerate(jobs):
        ahead = i + n_slots - 1
        if ahead < len(jobs):
            for c in copies(ahead):
                c.start()
        for c in copies(i):
            c.wait()
        convert(stage_ref.at[i % n_slots])


def _mixer_kernel(x_ref, g_ref, w_a2_ref, b_a_ref,
                  sgu_w_ref, sgu_bias_ref, ln_g_ref, ln_b_ref, gnorm_ref,
                  w_up_src_ref, w_down_src_ref,
                  w_in_t_hbm, w_o_sgu_hbm, w_o_gla_hbm, w_out_hbm,
                  o_ref, w_up_dst_ref, w_down_dst_ref, state_ref, su_ref, og_ref,
                  w_main_ref, w_tail_ref, w_o_sgu_ref, w_o_gla_ref, w_out_ref,
                  stage_ref, sems, *, layer):
    tb = x_ref.shape[0]
    n_sgu = tb // SGU_CHUNK
    gb = GLA_BLOCK_CHUNKS * GLA_CHUNK
    n_blk = tb // gb
    half = gb // 2

    @pl.when(jnp.logical_and(pl.program_id(0) == 0, pl.program_id(1) == 0))
    def _():
        _load_mixer_weights(layer, w_in_t_hbm, w_o_sgu_hbm, w_o_gla_hbm, w_out_hbm,
                            w_main_ref, w_tail_ref, w_o_sgu_ref, w_o_gla_ref, w_out_ref,
                            stage_ref, sems)

    @pl.when(pl.program_id(1) == 0)
    def _():
        state_ref[...] = jnp.zeros_like(state_ref)

    w_up_dst_ref[...] = w_up_src_ref[...].astype(BF16)
    w_down_dst_ref[...] = w_down_src_ref[...].astype(BF16)

    x = x_ref[...]
    h = _rmsnorm(x, g_ref[...]).astype(BF16)

    def proj(off, width):
        return _dot(h, w_main_ref[:, off:off + width])

    kw = GLA_KEY_WIDTH
    t = {}

    def project_u():
        t["z_u"] = proj(OFF_ZU, SGU_WIDTH)

    def project_v():
        t["z_v"] = proj(OFF_ZV, SGU_WIDTH)

    def gelu_u():
        t["u"] = jax.nn.gelu(t["z_u"])

    def project_tail():
        t["tail"] = _dot(h, w_tail_ref[...])
        t["a_low"] = t["tail"][:, 0:LANES].astype(BF16)

    def norm_v():
        v = jax.nn.gelu(t["z_v"])
        mu = jnp.mean(v, axis=-1, keepdims=True)
        vc = v - mu
        var = jnp.mean(vc * vc, axis=-1, keepdims=True)
        t["vn"] = (vc * lax.rsqrt(var + EPS) * ln_g_ref[...] + ln_b_ref[...]).astype(BF16)

    def log_decay():
        z = _dot(t["a_low"], w_a2_ref[...]) + b_a_ref[...]
        t["log_a"] = jax.nn.log_sigmoid(z) * (1.0 / GLA_GATE_TEMP)

    ri = lax.broadcasted_iota(jnp.int32, (gb, gb), 0)
    ci = lax.broadcasted_iota(jnp.int32, (gb, gb), 1)
    chunk_gap = ri // GLA_CHUNK - ci // GLA_CHUNK
    intra = jnp.logical_and(chunk_gap == 0, ci <= ri)
    adjacent = chunk_gap == 1
    far = chunk_gap[half:, :half] >= 2

    def prefix_sums():
        prefix_sum = jnp.where(intra, 1.0, 0.0).astype(BF16)
        log_a_split = jnp.concatenate(_split3(t["log_a"]), axis=1)
        bcums = []
        for blk in range(n_blk):
            parts = _dot(prefix_sum, log_a_split[blk * gb:(blk + 1) * gb])
            bcums.append(parts[:, 0:kw] + parts[:, kw:2 * kw] + parts[:, 2 * kw:3 * kw])
        t["bcums"] = bcums

    def project_qk():
        t["q"] = proj(OFF_Q, GLA_KEY_WIDTH) * (GLA_DK ** -0.5)
        t["k"] = proj(OFF_K, GLA_KEY_WIDTH)

    def sgu():
        row = lax.broadcasted_iota(jnp.int32, (SGU_CHUNK, SGU_CHUNK), 0)
        col = lax.broadcasted_iota(jnp.int32, (SGU_CHUNK, SGU_CHUNK), 1)
        causal = row >= col
        vn, u = t["vn"], t["u"]
        for g in range(SGU_GROUPS):
            cols = slice(g * SGU_GROUP_DIM, (g + 1) * SGU_GROUP_DIM)
            wg = jnp.where(causal, sgu_w_ref[g], 0.0).astype(BF16)
            vg = jnp.concatenate(
                [vn[c * SGU_CHUNK:(c + 1) * SGU_CHUNK, cols] for c in range(n_sgu)], axis=1)
            svg = _dot(wg, vg)
            bias = sgu_bias_ref[:, cols]
            for c in range(n_sgu):
                rows = slice(c * SGU_CHUNK, (c + 1) * SGU_CHUNK)
                sv = svg[:, c * SGU_GROUP_DIM:(c + 1) * SGU_GROUP_DIM] + bias
                su_ref[rows, cols] = (u[rows, cols] * sv).astype(BF16)

    def project_vv():
        t["vv"] = proj(OFF_V, GLA_VAL_WIDTH).astype(BF16)

    def project_r():
        t["r"] = proj(OFF_R, GLA_VAL_WIDTH)

    def gates():
        tail = t["tail"]
        wrapped = lax.broadcasted_iota(jnp.int32, (tb, LANES), 1) < GATE_SHIFT

        def rotated_gate(base):
            first = jnp.where(wrapped, tail[:, base + D_MODEL:base + D_MODEL + LANES],
                              tail[:, base:base + LANES])
            return jnp.concatenate([first, tail[:, base + LANES:base + D_MODEL]], axis=1)

        t["gate_sgu"] = jax.nn.sigmoid(rotated_gate(0))
        t["gate_gla"] = jax.nn.sigmoid(rotated_gate(D_MODEL))

    def project_sgu_out():
        t["y_sgu"] = _dot(su_ref[...], w_o_sgu_ref[...])

    def per_chunk_rows(rows):
        return jnp.concatenate([jnp.broadcast_to(r, (GLA_CHUNK, kw)) for r in rows], axis=0)

    def dot_nt(a, b):
        return lax.dot_general(a, b, (((1,), (1,)), ((), ())), preferred_element_type=F32)

    def gla_block(blk):
        one = jnp.ones((1, kw), F32)
        vv, r = t["vv"], t["r"]
        rows = slice(blk * gb, (blk + 1) * gb)
        bcum = t["bcums"][blk]
        q_b = t["q"][rows]
        k_b = t["k"][rows]
        bl = [bcum[(c + 1) * GLA_CHUNK - 1:(c + 1) * GLA_CHUNK, :] for c in range(GLA_BLOCK_CHUNKS)]
        c2 = bl[0] + bl[1]
        c3 = c2 + bl[2]
        c4 = c3 + bl[3]

        q_e32 = q_b * jnp.exp(bcum)
        k_state32 = k_b * jnp.exp(per_chunk_rows(bl) - bcum)
        q_e = q_e32.astype(BF16)
        k_intra = (k_b * jnp.exp(-bcum)).astype(BF16)
        k_state = k_state32.astype(BF16)
        q_blk = (q_e32 * per_chunk_rows(
            [one, jnp.exp(bl[0]), jnp.exp(c2), jnp.exp(c3)])).astype(BF16)
        k_blk = (k_state32 * per_chunk_rows(
            [jnp.exp(bl[1] + bl[2] + bl[3]), jnp.exp(bl[2] + bl[3]), jnp.exp(bl[3]), one])
                 ).astype(BF16)
        q_far = (q_e32[half:] * per_chunk_rows([one, jnp.exp(bl[2])])).astype(BF16)
        k_far = (k_state32[:half] * per_chunk_rows([jnp.exp(bl[1]), one])).astype(BF16)
        block_decay = jnp.exp(
            jnp.concatenate([c4, jnp.zeros((SUBLANES - 1, kw), F32)], axis=0))

        for hd in range(GLA_HEADS):
            kc = slice(hd * GLA_DK, (hd + 1) * GLA_DK)
            vcols = slice(hd * GLA_DV, (hd + 1) * GLA_DV)
            att = jnp.where(intra, dot_nt(q_e[:, kc], k_intra[:, kc]),
                            jnp.where(adjacent, dot_nt(q_e[:, kc], k_state[:, kc]), 0.0))
            att_far = jnp.where(far, dot_nt(q_far[:, kc], k_far[:, kc]), 0.0)
            att = jnp.concatenate(
                [att[:half],
                 jnp.concatenate([att[half:, :half] + att_far, att[half:, half:]], axis=1)],
                axis=0).astype(BF16)
            v_h = vv[rows, vcols]
            state = state_ref[hd]
            o = _dot(att, v_h) + _dot(q_blk[:, kc], state.astype(BF16))
            kv = lax.dot_general(k_blk[:, kc], v_h, (((0,), (0,)), ((), ())),
                                 preferred_element_type=F32)
            state_ref[hd] = state * block_decay[:, kc].T[:, 0:1] + kv
            o = _rmsnorm(o, gnorm_ref[...])
            og_ref[rows, vcols] = (o * jax.nn.silu(r[rows, vcols])).astype(BF16)

    gla_blocks = [functools.partial(gla_block, blk) for blk in range(n_blk)]

    def project_gla_out():
        t["y_gla"] = _dot(og_ref[...], w_o_gla_ref[...])

    def merge_and_project():
        merged = (t["gate_sgu"] * t["y_sgu"] + t["gate_gla"] * t["y_gla"]).astype(BF16)
        o_ref[...] = x + _dot(merged, w_out_ref[...])

    assert n_blk == 2
    for stage in (project_u, gelu_u, project_v, norm_v, project_tail, project_qk, log_decay,
                  project_vv, project_r, sgu, prefix_sums, gates, project_sgu_out,
                  gla_blocks[0], gla_blocks[1], project_gla_out, merge_and_project):
        stage()


def _ffn_kernel(x_ref, g_ref, w_up_ref, conv_w_ref, conv_b_ref, w_down_ref, final_g_ref,
                o_ref, hbuf_ref, *, final_norm):
    tb = x_ref.shape[0]
    pad = SUBLANES
    f = FFN_HIDDEN

    @pl.when(pl.program_id(1) == 0)
    def _():
        hbuf_ref[0:pad, :] = jnp.zeros((pad, 2 * f), F32)

    x = x_ref[...]
    hn = _rmsnorm(x, g_ref[...]).astype(BF16)

    hbuf_ref[pad:pad + tb, :] = _dot(hn, w_up_ref[...])
    hc = conv_b_ref[...]
    for j in range(CONV_WIDTH):
        shift = CONV_WIDTH - 1 - j
        hc = hc + hbuf_ref[pad - shift:pad - shift + tb, :] * conv_w_ref[j:j + 1, :]
    hbuf_ref[0:pad, :] = hbuf_ref[tb:tb + pad, :]
    a = hc[:, 0:f]
    b = hc[:, f:2 * f]
    y = x + _dot((jax.nn.silu(a) * b).astype(BF16), w_down_ref[...])
    if final_norm:
        y = _rmsnorm(y, final_g_ref[...])
    o_ref[...] = y


def _resident(shape):
    zeros = (0,) * len(shape)
    return pl.BlockSpec(shape, lambda b, s: zeros, pipeline_mode=pl.Buffered(1))


def _token_block(tb):
    return pl.BlockSpec((None, tb, D_MODEL), lambda b, s: (b, s, 0))


def _mixer_call(x, weights, layer, w_up, w_down, w_in_t, w_o_sgu, w_o_gla, w_out):
    batch, seq, _ = x.shape
    tb = MIXER_BLOCK
    steps = seq // tb
    assert seq % tb == 0 and tb % SGU_CHUNK == 0 and tb % (GLA_BLOCK_CHUNKS * GLA_CHUNK) == 0

    def cast_slices(w):
        rows, cols = w.shape[1:]
        assert rows % (steps * BF16_ROWS) == 0

        def slice_of(b, s):
            return jnp.minimum(b * steps + s, steps - 1)

        return (pl.BlockSpec((None, rows // steps, cols), lambda b, s: (layer, slice_of(b, s), 0)),
                pl.BlockSpec((rows // steps, cols), lambda b, s: (slice_of(b, s), 0)),
                jax.ShapeDtypeStruct((rows, cols), BF16))

    up_in, up_out, up_shape = cast_slices(w_up)
    down_in, down_out, down_shape = cast_slices(w_down)
    in_hbm = pl.BlockSpec(memory_space=pl.ANY)
    d = D_MODEL
    return pl.pallas_call(
        functools.partial(_mixer_kernel, layer=layer),
        grid=(batch, steps),
        in_specs=[_token_block(tb)] + [_resident(w.shape) for w in weights]
        + [up_in, down_in] + [in_hbm] * 4,
        out_specs=[_token_block(tb), up_out, down_out],
        out_shape=[jax.ShapeDtypeStruct(x.shape, x.dtype), up_shape, down_shape],
        scratch_shapes=[
            pltpu.VMEM((GLA_HEADS, GLA_DK, GLA_DV), F32),
            pltpu.VMEM((tb, SGU_WIDTH), BF16),
            pltpu.VMEM((tb, GLA_VAL_WIDTH), BF16),
            pltpu.VMEM((d, MAIN_COLS), BF16),
            pltpu.VMEM((d, TAIL_COLS), BF16),
            pltpu.VMEM((d, d), BF16),
            pltpu.VMEM((d, d), BF16),
            pltpu.VMEM((d, d), BF16),
            pltpu.VMEM((WEIGHT_STAGE_SLOTS, WEIGHT_STAGE_ROWS, d), F32),
            pltpu.SemaphoreType.DMA((WEIGHT_STAGE_SLOTS, 2)),
        ],
        compiler_params=pltpu.CompilerParams(
            dimension_semantics=("arbitrary", "arbitrary"),
            vmem_limit_bytes=VMEM_LIMIT_BYTES),
        name="mixer",
    )(x, *weights, w_up, w_down, w_in_t, w_o_sgu, w_o_gla, w_out)


def _ffn_call(x, weights, final_norm):
    batch, seq, _ = x.shape
    tb = FFN_BLOCK
    assert seq % tb == 0
    return pl.pallas_call(
        functools.partial(_ffn_kernel, final_norm=final_norm),
        grid=(batch, seq // tb),
        in_specs=[_token_block(tb)] + [_resident(w.shape) for w in weights],
        out_specs=_token_block(tb),
        out_shape=jax.ShapeDtypeStruct(x.shape, x.dtype),
        scratch_shapes=[pltpu.VMEM((tb + SUBLANES, 2 * FFN_HIDDEN), F32)],
        compiler_params=pltpu.CompilerParams(
            dimension_semantics=("arbitrary", "arbitrary"),
            vmem_limit_bytes=VMEM_LIMIT_BYTES),
        name="ffn_final" if final_norm else "ffn",
    )(x, *weights)


def _row(v):
    return v.reshape(1, -1).astype(F32)


def kernel(x, norm_mix_g, w_in, w_a2, b_a, sgu_w, sgu_b, sgu_ln_g, sgu_ln_b, gla_norm_g,
           w_o_sgu, w_o_gla, w_out, norm_ffn_g, w_up, conv_w, conv_b, w_down, final_norm_g):
    depth = w_in.shape[0]
    assert w_in.shape[2] == IN_COLS
    w_in_t = jnp.swapaxes(w_in, 1, 2)
    for l in range(depth):
        w_a2p = jnp.pad(w_a2[l], ((0, LANES - GLA_GATE_RANK), (0, 0)))
        sgu_bias = jnp.repeat(sgu_b[l].T, SGU_GROUP_DIM, axis=1)
        mixer_weights = (
            _row(norm_mix_g[l]),
            w_a2p.astype(BF16),
            _row(b_a[l]),
            sgu_w[l],
            sgu_bias,
            _row(sgu_ln_g[l]),
            _row(sgu_ln_b[l]),
            _row(gla_norm_g[l]),
        )
        x, w_up_l, w_down_l = _mixer_call(x, mixer_weights, l, w_up, w_down,
                                          w_in_t, w_o_sgu, w_o_gla, w_out)
        ffn_weights = (
            _row(norm_ffn_g[l]),
            w_up_l,
            conv_w[l],
            _row(conv_b[l]),
            w_down_l,
            _row(final_norm_g),
        )
        x = _ffn_call(x, ffn_weights, final_norm=(l == depth - 1))
    return x
```

```python
import functools

import jax
import jax.numpy as jnp
from jax import lax
from jax.experimental import pallas as pl
from jax.experimental.pallas import tpu as pltpu

D_MODEL = 1024
SGU_CHUNK = 128
SGU_GROUPS = 8
SGU_GROUP_DIM = 128
SGU_WIDTH = SGU_GROUPS * SGU_GROUP_DIM
GLA_HEADS = 4
GLA_DK = 128
GLA_DV = 256
GLA_KEY_WIDTH = GLA_HEADS * GLA_DK
GLA_VAL_WIDTH = GLA_HEADS * GLA_DV
GLA_GATE_RANK = 16
GLA_GATE_TEMP = 16.0
GLA_CHUNK = 64
FFN_HIDDEN = 2816
CONV_WIDTH = 3
EPS = 1e-6

LANES = 128
SUBLANES = 8
BF16_ROWS = 16
VMEM_LIMIT_BYTES = 56 * 1024 * 1024

OFF_ZU = 0
OFF_ZV = OFF_ZU + SGU_WIDTH
OFF_Q = OFF_ZV + SGU_WIDTH
OFF_K = OFF_Q + GLA_KEY_WIDTH
OFF_V = OFF_K + GLA_KEY_WIDTH
OFF_R = OFF_V + GLA_VAL_WIDTH
MAIN_COLS = OFF_R + GLA_VAL_WIDTH
IN_COLS = MAIN_COLS + GLA_GATE_RANK + 2 * D_MODEL
TAIL_ALIGNED = (IN_COLS - MAIN_COLS) // LANES * LANES
TAIL_COLS = TAIL_ALIGNED + LANES
GATE_SHIFT = GLA_GATE_RANK

MIXER_BLOCK = 512
FFN_BLOCK = 512
GLA_BLOCK_CHUNKS = 4
WEIGHT_STAGE_ROWS = 256
WEIGHT_STAGE_SLOTS = 3

F32 = jnp.float32
BF16 = jnp.bfloat16

def _dot(a, b):
    return jnp.dot(a, b, preferred_element_type=F32)


def _rmsnorm(x, g):
    return x * lax.rsqrt(jnp.mean(x * x, axis=-1, keepdims=True) + EPS) * g


def _split3(x):
    hi = x.astype(BF16)
    r1 = x - hi.astype(F32)
    mid = r1.astype(BF16)
    lo = (r1 - mid.astype(F32)).astype(BF16)
    return hi, mid, lo


def _load_mixer_weights(layer, w_in_t_hbm, w_o_sgu_hbm, w_o_gla_hbm, w_out_hbm,
                        w_main_ref, w_tail_ref, w_o_sgu_ref, w_o_gla_ref, w_out_ref,
                        stage_ref, sems):
    r = stage_ref.shape[1]
    d = D_MODEL
    jobs = []

    def transposed_into(col):
        def convert(slot_ref):
            block = slot_ref[...].T.astype(BF16)
            if col < MAIN_COLS:
                w_main_ref[:, col:col + r] = block
            else:
                w_tail_ref[:, col - MAIN_COLS:col - MAIN_COLS + r] = block
        return convert

    n_full = IN_COLS // r
    rest = IN_COLS - n_full * r
    assert MAIN_COLS % r == 0 and n_full * r == MAIN_COLS + TAIL_ALIGNED and rest <= LANES
    for j in range(n_full):
        jobs.append(([(w_in_t_hbm.at[layer, pl.ds(j * r, r), :], 0, r)],
                     transposed_into(j * r)))

    def tail_end(slot_ref):
        rows = jnp.concatenate(
            [slot_ref[0:rest, :], jnp.zeros((LANES - rest, d), F32)], axis=0)
        w_tail_ref[:, TAIL_ALIGNED:TAIL_COLS] = rows.T.astype(BF16)

    jobs.append(([(w_in_t_hbm.at[layer, pl.ds(n_full * r, rest), :], 0, rest)], tail_end))

    def rotated_rows_into(dst_ref, row):
        def convert(slot_ref):
            dst_ref[row:row + r, :] = pltpu.roll(slot_ref[...], GATE_SHIFT, axis=1).astype(BF16)
        return convert

    for src, dst in ((w_o_sgu_hbm, w_o_sgu_ref), (w_o_gla_hbm, w_o_gla_ref)):
        for row in range(0, d, r):
            jobs.append(([(src.at[layer, pl.ds(row, r), :], 0, r)],
                         rotated_rows_into(dst, row)))

    def rows_into(row):
        def convert(slot_ref):
            w_out_ref[row:row + r, :] = slot_ref[...].astype(BF16)
        return convert

    s = GATE_SHIFT
    jobs.append(([(w_out_hbm.at[layer, pl.ds(d - s, s), :], 0, s),
                  (w_out_hbm.at[layer, pl.ds(0, r - s), :], s, r - s)], rows_into(0)))
    for row in range(r, d, r):
        jobs.append(([(w_out_hbm.at[layer, pl.ds(row - s, r), :], 0, r)], rows_into(row)))

    n_slots = stage_ref.shape[0]

    def copies(i):
        slot = i % n_slots
        return [pltpu.make_async_copy(src, stage_ref.at[slot, pl.ds(off, n), :],
                                      sems.at[slot, p])
                for p, (src, off, n) in enumerate(jobs[i][0])]

    for i in range(min(n_slots - 1, len(jobs))):
        for c in copies(i):
            c.start()
    for i, (_, convert) in enumerate(jobs):
        ahead = i + n_slots - 1
        if ahead < len(jobs):
            for c in copies(ahead):
                c.start()
        for c in copies(i):
            c.wait()
        convert(stage_ref.at[i % n_slots])


def _mixer_kernel(x_ref, g_ref, w_a2_ref, b_a_ref,
                  sgu_w_ref, sgu_bias_ref, ln_g_ref, ln_b_ref, gnorm_ref,
                  w_up_src_ref, w_down_src_ref,
                  w_in_t_hbm, w_o_sgu_hbm, w_o_gla_hbm, w_out_hbm,
                  o_ref, w_up_dst_ref, w_down_dst_ref, state_ref, su_ref, og_ref,
                  w_main_ref, w_tail_ref, w_o_sgu_ref, w_o_gla_ref, w_out_ref,
                  stage_ref, sems, *, layer):
    tb = x_ref.shape[0]
    n_sgu = tb // SGU_CHUNK
    gb = GLA_BLOCK_CHUNKS * GLA_CHUNK
    n_blk = tb // gb
    half = gb // 2

    @pl.when(jnp.logical_and(pl.program_id(0) == 0, pl.program_id(1) == 0))
    def _():
        _load_mixer_weights(layer, w_in_t_hbm, w_o_sgu_hbm, w_o_gla_hbm, w_out_hbm,
                            w_main_ref, w_tail_ref, w_o_sgu_ref, w_o_gla_ref, w_out_ref,
                            stage_ref, sems)

    @pl.when(pl.program_id(1) == 0)
    def _():
        state_ref[...] = jnp.zeros_like(state_ref)

    w_up_dst_ref[...] = w_up_src_ref[...].astype(BF16)
    w_down_dst_ref[...] = w_down_src_ref[...].astype(BF16)

    x = x_ref[...]
    h = _rmsnorm(x, g_ref[...]).astype(BF16)

    def proj(off, width):
        return _dot(h, w_main_ref[:, off:off + width])

    kw = GLA_KEY_WIDTH
    t = {}

    def project_u():
        t["z_u"] = proj(OFF_ZU, SGU_WIDTH)

    def project_v():
        t["z_v"] = proj(OFF_ZV, SGU_WIDTH)

    def gelu_u():
        t["u"] = jax.nn.gelu(t["z_u"])

    def project_tail():
        t["tail"] = _dot(h, w_tail_ref[...])
        t["a_low"] = t["tail"][:, 0:LANES].astype(BF16)

    def norm_v():
        v = jax.nn.gelu(t["z_v"])
        mu = jnp.mean(v, axis=-1, keepdims=True)
        vc = v - mu
        var = jnp.mean(vc * vc, axis=-1, keepdims=True)
        t["vn"] = (vc * lax.rsqrt(var + EPS) * ln_g_ref[...] + ln_b_ref[...]).astype(BF16)

    def log_decay():
        z = _dot(t["a_low"], w_a2_ref[...]) + b_a_ref[...]
        t["log_a"] = jax.nn.log_sigmoid(z) * (1.0 / GLA_GATE_TEMP)

    ri = lax.broadcasted_iota(jnp.int32, (gb, gb), 0)
    ci = lax.broadcasted_iota(jnp.int32, (gb, gb), 1)
    chunk_gap = ri // GLA_CHUNK - ci // GLA_CHUNK
    intra = jnp.logical_and(chunk_gap == 0, ci <= ri)
    adjacent = chunk_gap == 1
    far = chunk_gap[half:, :half] >= 2

    def prefix_sums():
        prefix_sum = jnp.where(intra, 1.0, 0.0).astype(BF16)
        log_a_split = jnp.concatenate(_split3(t["log_a"]), axis=1)
        bcums = []
        for blk in range(n_blk):
            parts = _dot(prefix_sum, log_a_split[blk * gb:(blk + 1) * gb])
            bcums.append(parts[:, 0:kw] + parts[:, kw:2 * kw] + parts[:, 2 * kw:3 * kw])
        t["bcums"] = bcums

    def project_qk():
        t["q"] = proj(OFF_Q, GLA_KEY_WIDTH) * (GLA_DK ** -0.5)
        t["k"] = proj(OFF_K, GLA_KEY_WIDTH)

    def sgu():
        row = lax.broadcasted_iota(jnp.int32, (SGU_CHUNK, SGU_CHUNK), 0)
        col = lax.broadcasted_iota(jnp.int32, (SGU_CHUNK, SGU_CHUNK), 1)
        causal = row >= col
        vn, u = t["vn"], t["u"]
        for g in range(SGU_GROUPS):
            cols = slice(g * SGU_GROUP_DIM, (g + 1) * SGU_GROUP_DIM)
            wg = jnp.where(causal, sgu_w_ref[g], 0.0).astype(BF16)
            vg = jnp.concatenate(
                [vn[c * SGU_CHUNK:(c + 1) * SGU_CHUNK, cols] for c in range(n_sgu)], axis=1)
            svg = _dot(wg, vg)
            bias = sgu_bias_ref[:, cols]
            for c in range(n_sgu):
                rows = slice(c * SGU_CHUNK, (c + 1) * SGU_CHUNK)
                sv = svg[:, c * SGU_GROUP_DIM:(c + 1) * SGU_GROUP_DIM] + bias
                su_ref[rows, cols] = (u[rows, cols] * sv).astype(BF16)

    def project_vv():
        t["vv"] = proj(OFF_V, GLA_VAL_WIDTH).astype(BF16)

    def project_r():
        t["r"] = proj(OFF_R, GLA_VAL_WIDTH)

    def gates():
        tail = t["tail"]
        wrapped = lax.broadcasted_iota(jnp.int32, (tb, LANES), 1) < GATE_SHIFT

        def rotated_gate(base):
            first = jnp.where(wrapped, tail[:, base + D_MODEL:base + D_MODEL + LANES],
                              tail[:, base:base + LANES])
            return jnp.concatenate([first, tail[:, base + LANES:base + D_MODEL]], axis=1)

        t["gate_sgu"] = jax.nn.sigmoid(rotated_gate(0))
        t["gate_gla"] = jax.nn.sigmoid(rotated_gate(D_MODEL))

    def project_sgu_out():
        t["y_sgu"] = _dot(su_ref[...], w_o_sgu_ref[...])

    def per_chunk_rows(rows):
        return jnp.concatenate([jnp.broadcast_to(r, (GLA_CHUNK, kw)) for r in rows], axis=0)

    def dot_nt(a, b):
        return lax.dot_general(a, b, (((1,), (1,)), ((), ())), preferred_element_type=F32)

    def gla_block(blk):
        one = jnp.ones((1, kw), F32)
        vv, r = t["vv"], t["r"]
        rows = slice(blk * gb, (blk + 1) * gb)
        bcum = t["bcums"][blk]
        q_b = t["q"][rows]
        k_b = t["k"][rows]
        bl = [bcum[(c + 1) * GLA_CHUNK - 1:(c + 1) * GLA_CHUNK, :] for c in range(GLA_BLOCK_CHUNKS)]
        c2 = bl[0] + bl[1]
        c3 = c2 + bl[2]
        c4 = c3 + bl[3]

        q_e32 = q_b * jnp.exp(bcum)
        k_state32 = k_b * jnp.exp(per_chunk_rows(bl) - bcum)
        q_e = q_e32.astype(BF16)
        k_intra = (k_b * jnp.exp(-bcum)).astype(BF16)
        k_state = k_state32.astype(BF16)
        q_blk = (q_e32 * per_chunk_rows(
            [one, jnp.exp(bl[0]), jnp.exp(c2), jnp.exp(c3)])).astype(BF16)
        k_blk = (k_state32 * per_chunk_rows(
            [jnp.exp(bl[1] + bl[2] + bl[3]), jnp.exp(bl[2] + bl[3]), jnp.exp(bl[3]), one])
                 ).astype(BF16)
        q_far = (q_e32[half:] * per_chunk_rows([one, jnp.exp(bl[2])])).astype(BF16)
        k_far = (k_state32[:half] * per_chunk_rows([jnp.exp(bl[1]), one])).astype(BF16)
        block_decay = jnp.exp(
            jnp.concatenate([c4, jnp.zeros((SUBLANES - 1, kw), F32)], axis=0))

        for hd in range(GLA_HEADS):
            kc = slice(hd * GLA_DK, (hd + 1) * GLA_DK)
            vcols = slice(hd * GLA_DV, (hd + 1) * GLA_DV)
            att = jnp.where(intra, dot_nt(q_e[:, kc], k_intra[:, kc]),
                            jnp.where(adjacent, dot_nt(q_e[:, kc], k_state[:, kc]), 0.0))
            att_far = jnp.where(far, dot_nt(q_far[:, kc], k_far[:, kc]), 0.0)
            att = jnp.concatenate(
                [att[:half],
                 jnp.concatenate([att[half:, :half] + att_far, att[half:, half:]], axis=1)],
                axis=0).astype(BF16)
            v_h = vv[rows, vcols]
            state = state_ref[hd]
            o = _dot(att, v_h) + _dot(q_blk[:, kc], state.astype(BF16))
            kv = lax.dot_general(k_blk[:, kc], v_h, (((0,), (0,)), ((), ())),
                                 preferred_element_type=F32)
            state_ref[hd] = state * block_decay[:, kc].T[:, 0:1] + kv
            o = _rmsnorm(o, gnorm_ref[...])
            og_ref[rows, vcols] = (o * jax.nn.silu(r[rows, vcols])).astype(BF16)

    gla_blocks = [functools.partial(gla_block, blk) for blk in range(n_blk)]

    def project_gla_out():
        t["y_gla"] = _dot(og_ref[...], w_o_gla_ref[...])

    def merge_and_project():
        merged = (t["gate_sgu"] * t["y_sgu"] + t["gate_gla"] * t["y_gla"]).astype(BF16)
        o_ref[...] = x + _dot(merged, w_out_ref[...])

    assert n_blk == 2
    for stage in (project_u, gelu_u, project_v, norm_v, project_tail, project_qk, log_decay,
                  project_vv, project_r, sgu, prefix_sums, gates, project_sgu_out,
                  gla_blocks[0], gla_blocks[1], project_gla_out, merge_and_project):
        stage()


def _ffn_kernel(x_ref, g_ref, w_up_ref, conv_w_ref, conv_b_ref, w_down_ref, final_g_ref,
                o_ref, hbuf_ref, *, final_norm):
    tb = x_ref.shape[0]
    pad = SUBLANES
    f = FFN_HIDDEN

    @pl.when(pl.program_id(1) == 0)
    def _():
        hbuf_ref[0:pad, :] = jnp.zeros((pad, 2 * f), F32)

    x = x_ref[...]
    hn = _rmsnorm(x, g_ref[...]).astype(BF16)

    hbuf_ref[pad:pad + tb, :] = _dot(hn, w_up_ref[...])
    hc = conv_b_ref[...]
    for j in range(CONV_WIDTH):
        shift = CONV_WIDTH - 1 - j
        hc = hc + hbuf_ref[pad - shift:pad - shift + tb, :] * conv_w_ref[j:j + 1, :]
    hbuf_ref[0:pad, :] = hbuf_ref[tb:tb + pad, :]
    a = hc[:, 0:f]
    b = hc[:, f:2 * f]
    y = x + _dot((jax.nn.silu(a) * b).astype(BF16), w_down_ref[...])
    if final_norm:
        y = _rmsnorm(y, final_g_ref[...])
    o_ref[...] = y


def _resident(shape):
    zeros = (0,) * len(shape)
    return pl.BlockSpec(shape, lambda b, s: zeros, pipeline_mode=pl.Buffered(1))


def _token_block(tb):
    return pl.BlockSpec((None, tb, D_MODEL), lambda b, s: (b, s, 0))


def _mixer_call(x, weights, layer, w_up, w_down, w_in_t, w_o_sgu, w_o_gla, w_out):
    batch, seq, _ = x.shape
    tb = MIXER_BLOCK
    steps = seq // tb
    assert seq % tb == 0 and tb % SGU_CHUNK == 0 and tb % (GLA_BLOCK_CHUNKS * GLA_CHUNK) == 0

    def cast_slices(w):
        rows, cols = w.shape[1:]
        assert rows % (steps * BF16_ROWS) == 0

        def slice_of(b, s):
            return jnp.minimum(b * steps + s, steps - 1)

        return (pl.BlockSpec((None, rows // steps, cols), lambda b, s: (layer, slice_of(b, s), 0)),
                pl.BlockSpec((rows // steps, cols), lambda b, s: (slice_of(b, s), 0)),
                jax.ShapeDtypeStruct((rows, cols), BF16))

    up_in, up_out, up_shape = cast_slices(w_up)
    down_in, down_out, down_shape = cast_slices(w_down)
    in_hbm = pl.BlockSpec(memory_space=pl.ANY)
    d = D_MODEL
    return pl.pallas_call(
        functools.partial(_mixer_kernel, layer=layer),
        grid=(batch, steps),
        in_specs=[_token_block(tb)] + [_resident(w.shape) for w in weights]
        + [up_in, down_in] + [in_hbm] * 4,
        out_specs=[_token_block(tb), up_out, down_out],
        out_shape=[jax.ShapeDtypeStruct(x.shape, x.dtype), up_shape, down_shape],
        scratch_shapes=[
            pltpu.VMEM((GLA_HEADS, GLA_DK, GLA_DV), F32),
            pltpu.VMEM((tb, SGU_WIDTH), BF16),
            pltpu.VMEM((tb, GLA_VAL_WIDTH), BF16),
            pltpu.VMEM((d, MAIN_COLS), BF16),
            pltpu.VMEM((d, TAIL_COLS), BF16),
            pltpu.VMEM((d, d), BF16),
            pltpu.VMEM((d, d), BF16),
            pltpu.VMEM((d, d), BF16),
            pltpu.VMEM((WEIGHT_STAGE_SLOTS, WEIGHT_STAGE_ROWS, d), F32),
            pltpu.SemaphoreType.DMA((WEIGHT_STAGE_SLOTS, 2)),
        ],
        compiler_params=pltpu.CompilerParams(
            dimension_semantics=("arbitrary", "arbitrary"),
            vmem_limit_bytes=VMEM_LIMIT_BYTES),
        name="mixer",
    )(x, *weights, w_up, w_down, w_in_t, w_o_sgu, w_o_gla, w_out)


def _ffn_call(x, weights, final_norm):
    batch, seq, _ = x.shape
    tb = FFN_BLOCK
    assert seq % tb == 0
    return pl.pallas_call(
        functools.partial(_ffn_kernel, final_norm=final_norm),
        grid=(batch, seq // tb),
        in_specs=[_token_block(tb)] + [_resident(w.shape) for w in weights],
        out_specs=_token_block(tb),
        out_shape=jax.ShapeDtypeStruct(x.shape, x.dtype),
        scratch_shapes=[pltpu.VMEM((tb + SUBLANES, 2 * FFN_HIDDEN), F32)],
        compiler_params=pltpu.CompilerParams(
            dimension_semantics=("arbitrary", "arbitrary"),
            vmem_limit_bytes=VMEM_LIMIT_BYTES),
        name="ffn_final" if final_norm else "ffn",
    )(x, *weights)


def _row(v):
    return v.reshape(1, -1).astype(F32)


def kernel(x, norm_mix_g, w_in, w_a2, b_a, sgu_w, sgu_b, sgu_ln_g, sgu_ln_b, gla_norm_g,
           w_o_sgu, w_o_gla, w_out, norm_ffn_g, w_up, conv_w, conv_b, w_down, final_norm_g):
    depth = w_in.shape[0]
    assert w_in.shape[2] == IN_COLS
    w_in_t = jnp.swapaxes(w_in, 1, 2)
    for l in range(depth):
        w_a2p = jnp.pad(w_a2[l], ((0, LANES - GLA_GATE_RANK), (0, 0)))
        sgu_bias = jnp.repeat(sgu_b[l].T, SGU_GROUP_DIM, axis=1)
        mixer_weights = (
            _row(norm_mix_g[l]),
            w_a2p.astype(BF16),
            _row(b_a[l]),
            sgu_w[l],
            sgu_bias,
            _row(sgu_ln_g[l]),
            _row(sgu_ln_b[l]),
            _row(gla_norm_g[l]),
        )
        x, w_up_l, w_down_l = _mixer_call(x, mixer_weights, l, w_up, w_down,
                                          w_in_t, w_o_sgu, w_o_gla, w_out)
        ffn_weights = (
            _row(norm_ffn_g[l]),
            w_up_l,
            conv_w[l],
            _row(conv_b[l]),
            w_down_l,
            _row(final_norm_g),
        )
        x = _ffn_call(x, ffn_weights, final_norm=(l == depth - 1))
    return x
```

```python
import functools

import jax
import jax.numpy as jnp
from jax import lax
from jax.experimental import pallas as pl
from jax.experimental.pallas import tpu as pltpu

D_MODEL = 1024
SGU_CHUNK = 128
SGU_GROUPS = 8
SGU_GROUP_DIM = 128
SGU_WIDTH = SGU_GROUPS * SGU_GROUP_DIM
GLA_HEADS = 4
GLA_DK = 128
GLA_DV = 256
GLA_KEY_WIDTH = GLA_HEADS * GLA_DK
GLA_VAL_WIDTH = GLA_HEADS * GLA_DV
GLA_GATE_RANK = 16
GLA_GATE_TEMP = 16.0
GLA_CHUNK = 64
FFN_HIDDEN = 2816
CONV_WIDTH = 3
EPS = 1e-6

LANES = 128
SUBLANES = 8
BF16_ROWS = 16
VMEM_LIMIT_BYTES = 58 * 1024 * 1024

OFF_ZU = 0
OFF_ZV = OFF_ZU + SGU_WIDTH
OFF_Q = OFF_ZV + SGU_WIDTH
OFF_K = OFF_Q + GLA_KEY_WIDTH
OFF_V = OFF_K + GLA_KEY_WIDTH
OFF_R = OFF_V + GLA_VAL_WIDTH
MAIN_COLS = OFF_R + GLA_VAL_WIDTH
IN_COLS = MAIN_COLS + GLA_GATE_RANK + 2 * D_MODEL
TAIL_ALIGNED = (IN_COLS - MAIN_COLS) // LANES * LANES
TAIL_COLS = TAIL_ALIGNED + LANES
GATE_SHIFT = GLA_GATE_RANK

MIXER_BLOCK = 512
FFN_BLOCK = 512
GLA_BLOCK_CHUNKS = 4
WEIGHT_STAGE_ROWS = 256
WEIGHT_STAGE_SLOTS = 4

F32 = jnp.float32
BF16 = jnp.bfloat16

def _dot(a, b):
    return jnp.dot(a, b, preferred_element_type=F32)


def _rmsnorm(x, g):
    return x * lax.rsqrt(jnp.mean(x * x, axis=-1, keepdims=True) + EPS) * g


def _split3(x):
    hi = x.astype(BF16)
    r1 = x - hi.astype(F32)
    mid = r1.astype(BF16)
    lo = (r1 - mid.astype(F32)).astype(BF16)
    return hi, mid, lo


def _load_mixer_weights(layer, w_in_t_hbm, w_o_sgu_hbm, w_o_gla_hbm, w_out_hbm,
                        w_main_ref, w_tail_ref, w_o_sgu_ref, w_o_gla_ref, w_out_ref,
                        stage_ref, sems):
    r = stage_ref.shape[1]
    d = D_MODEL
    jobs = []

    def transposed_into(col):
        def convert(slot_ref):
            block = slot_ref[...].T.astype(BF16)
            if col < MAIN_COLS:
                w_main_ref[:, col:col + r] = block
            else:
                w_tail_ref[:, col - MAIN_COLS:col - MAIN_COLS + r] = block
        return convert

    n_full = IN_COLS // r
    rest = IN_COLS - n_full * r
    assert MAIN_COLS % r == 0 and n_full * r == MAIN_COLS + TAIL_ALIGNED and rest <= LANES
    for j in range(n_full):
        jobs.append(([(w_in_t_hbm.at[layer, pl.ds(j * r, r), :], 0, r)],
                     transposed_into(j * r)))

    def tail_end(slot_ref):
        rows = jnp.concatenate(
            [slot_ref[0:rest, :], jnp.zeros((LANES - rest, d), F32)], axis=0)
        w_tail_ref[:, TAIL_ALIGNED:TAIL_COLS] = rows.T.astype(BF16)

    jobs.append(([(w_in_t_hbm.at[layer, pl.ds(n_full * r, rest), :], 0, rest)], tail_end))

    def rotated_rows_into(dst_ref, row):
        def convert(slot_ref):
            dst_ref[row:row + r, :] = pltpu.roll(slot_ref[...], GATE_SHIFT, axis=1).astype(BF16)
        return convert

    for src, dst in ((w_o_sgu_hbm, w_o_sgu_ref), (w_o_gla_hbm, w_o_gla_ref)):
        for row in range(0, d, r):
            jobs.append(([(src.at[layer, pl.ds(row, r), :], 0, r)],
                         rotated_rows_into(dst, row)))

    def rows_into(row):
        def convert(slot_ref):
            w_out_ref[row:row + r, :] = slot_ref[...].astype(BF16)
        return convert

    s = GATE_SHIFT
    jobs.append(([(w_out_hbm.at[layer, pl.ds(d - s, s), :], 0, s),
                  (w_out_hbm.at[layer, pl.ds(0, r - s), :], s, r - s)], rows_into(0)))
    for row in range(r, d, r):
        jobs.append(([(w_out_hbm.at[layer, pl.ds(row - s, r), :], 0, r)], rows_into(row)))

    n_slots = stage_ref.shape[0]

    def copies(i):
        slot = i % n_slots
        return [pltpu.make_async_copy(src, stage_ref.at[slot, pl.ds(off, n), :],
                                      sems.at[slot, p])
                for p, (src, off, n) in enumerate(jobs[i][0])]

    for i in range(min(n_slots - 1, len(jobs))):
        for c in copies(i):
            c.start()
    for i, (_, convert) in enumerate(jobs):
        ahead = i + n_slots - 1
        if ahead < len(jobs):
            for c in copies(ahead):
                c.start()
        for c in copies(i):
            c.wait()
        convert(stage_ref.at[i % n_slots])


def _mixer_kernel(x_ref, g_ref, w_a2_ref, b_a_ref,
                  sgu_w_ref, sgu_bias_ref, ln_g_ref, ln_b_ref, gnorm_ref,
                  w_up_src_ref, w_down_src_ref,
                  w_in_t_hbm, w_o_sgu_hbm, w_o_gla_hbm, w_out_hbm,
                  o_ref, w_up_dst_ref, w_down_dst_ref, state_ref, su_ref, og_ref,
                  w_main_ref, w_tail_ref, w_o_sgu_ref, w_o_gla_ref, w_out_ref,
                  stage_ref, sems, *, layer):
    tb = x_ref.shape[0]
    n_sgu = tb // SGU_CHUNK
    gb = GLA_BLOCK_CHUNKS * GLA_CHUNK
    n_blk = tb // gb
    half = gb // 2

    @pl.when(jnp.logical_and(pl.program_id(0) == 0, pl.program_id(1) == 0))
    def _():
        _load_mixer_weights(layer, w_in_t_hbm, w_o_sgu_hbm, w_o_gla_hbm, w_out_hbm,
                            w_main_ref, w_tail_ref, w_o_sgu_ref, w_o_gla_ref, w_out_ref,
                            stage_ref, sems)

    @pl.when(pl.program_id(1) == 0)
    def _():
        state_ref[...] = jnp.zeros_like(state_ref)

    w_up_dst_ref[...] = w_up_src_ref[...].astype(BF16)
    w_down_dst_ref[...] = w_down_src_ref[...].astype(BF16)

    x = x_ref[...]
    h = _rmsnorm(x, g_ref[...]).astype(BF16)

    def proj(off, width):
        return _dot(h, w_main_ref[:, off:off + width])

    kw = GLA_KEY_WIDTH
    t = {}

    def project_u():
        t["z_u"] = proj(OFF_ZU, SGU_WIDTH)

    def project_v():
        t["z_v"] = proj(OFF_ZV, SGU_WIDTH)

    def gelu_u():
        t["u"] = jax.nn.gelu(t["z_u"])

    def project_tail():
        t["tail"] = _dot(h, w_tail_ref[...])
        t["a_low"] = t["tail"][:, 0:LANES].astype(BF16)

    def norm_v():
        v = jax.nn.gelu(t["z_v"])
        mu = jnp.mean(v, axis=-1, keepdims=True)
        vc = v - mu
        var = jnp.mean(vc * vc, axis=-1, keepdims=True)
        t["vn"] = (vc * lax.rsqrt(var + EPS) * ln_g_ref[...] + ln_b_ref[...]).astype(BF16)

    def log_decay():
        z = _dot(t["a_low"], w_a2_ref[...]) + b_a_ref[...]
        t["log_a"] = jax.nn.log_sigmoid(z) * (1.0 / GLA_GATE_TEMP)

    ri = lax.broadcasted_iota(jnp.int32, (gb, gb), 0)
    ci = lax.broadcasted_iota(jnp.int32, (gb, gb), 1)
    chunk_gap = ri // GLA_CHUNK - ci // GLA_CHUNK
    intra = jnp.logical_and(chunk_gap == 0, ci <= ri)
    adjacent = chunk_gap == 1
    far = chunk_gap[half:, :half] >= 2

    def prefix_sums():
        prefix_sum = jnp.where(intra, 1.0, 0.0).astype(BF16)
        log_a_split = jnp.concatenate(_split3(t["log_a"]), axis=1)
        bcums = []
        for blk in range(n_blk):
            parts = _dot(prefix_sum, log_a_split[blk * gb:(blk + 1) * gb])
            bcums.append(parts[:, 0:kw] + parts[:, kw:2 * kw] + parts[:, 2 * kw:3 * kw])
        t["bcums"] = bcums

    def project_qk():
        t["q"] = proj(OFF_Q, GLA_KEY_WIDTH) * (GLA_DK ** -0.5)
        t["k"] = proj(OFF_K, GLA_KEY_WIDTH)

    def sgu():
        row = lax.broadcasted_iota(jnp.int32, (SGU_CHUNK, SGU_CHUNK), 0)
        col = lax.broadcasted_iota(jnp.int32, (SGU_CHUNK, SGU_CHUNK), 1)
        causal = row >= col
        vn, u = t["vn"], t["u"]
        for g in range(SGU_GROUPS):
            cols = slice(g * SGU_GROUP_DIM, (g + 1) * SGU_GROUP_DIM)
            wg = jnp.where(causal, sgu_w_ref[g], 0.0).astype(BF16)
            vg = jnp.concatenate(
                [vn[c * SGU_CHUNK:(c + 1) * SGU_CHUNK, cols] for c in range(n_sgu)], axis=1)
            svg = _dot(wg, vg)
            bias = sgu_bias_ref[:, cols]
            for c in range(n_sgu):
                rows = slice(c * SGU_CHUNK, (c + 1) * SGU_CHUNK)
                sv = svg[:, c * SGU_GROUP_DIM:(c + 1) * SGU_GROUP_DIM] + bias
                su_ref[rows, cols] = (u[rows, cols] * sv).astype(BF16)

    def project_vv():
        t["vv"] = proj(OFF_V, GLA_VAL_WIDTH).astype(BF16)

    def project_r():
        t["r"] = proj(OFF_R, GLA_VAL_WIDTH)

    def gates():
        tail = t["tail"]
        wrapped = lax.broadcasted_iota(jnp.int32, (tb, LANES), 1) < GATE_SHIFT

        def rotated_gate(base):
            first = jnp.where(wrapped, tail[:, base + D_MODEL:base + D_MODEL + LANES],
                              tail[:, base:base + LANES])
            return jnp.concatenate([first, tail[:, base + LANES:base + D_MODEL]], axis=1)

        t["gate_sgu"] = jax.nn.sigmoid(rotated_gate(0))
        t["gate_gla"] = jax.nn.sigmoid(rotated_gate(D_MODEL))

    def project_sgu_out():
        t["y_sgu"] = _dot(su_ref[...], w_o_sgu_ref[...])

    def per_chunk_rows(rows):
        return jnp.concatenate([jnp.broadcast_to(r, (GLA_CHUNK, kw)) for r in rows], axis=0)

    def dot_nt(a, b):
        return lax.dot_general(a, b, (((1,), (1,)), ((), ())), preferred_element_type=F32)

    def gla_block(blk):
        one = jnp.ones((1, kw), F32)
        vv, r = t["vv"], t["r"]
        rows = slice(blk * gb, (blk + 1) * gb)
        bcum = t["bcums"][blk]
        q_b = t["q"][rows]
        k_b = t["k"][rows]
        bl = [bcum[(c + 1) * GLA_CHUNK - 1:(c + 1) * GLA_CHUNK, :] for c in range(GLA_BLOCK_CHUNKS)]
        c2 = bl[0] + bl[1]
        c3 = c2 + bl[2]
        c4 = c3 + bl[3]

        q_e32 = q_b * jnp.exp(bcum)
        k_state32 = k_b * jnp.exp(per_chunk_rows(bl) - bcum)
        q_e = q_e32.astype(BF16)
        k_intra = (k_b * jnp.exp(-bcum)).astype(BF16)
        k_state = k_state32.astype(BF16)
        q_blk = (q_e32 * per_chunk_rows(
            [one, jnp.exp(bl[0]), jnp.exp(c2), jnp.exp(c3)])).astype(BF16)
        k_blk = (k_state32 * per_chunk_rows(
            [jnp.exp(bl[1] + bl[2] + bl[3]), jnp.exp(bl[2] + bl[3]), jnp.exp(bl[3]), one])
                 ).astype(BF16)
        q_far = (q_e32[half:] * per_chunk_rows([one, jnp.exp(bl[2])])).astype(BF16)
        k_far = (k_state32[:half] * per_chunk_rows([jnp.exp(bl[1]), one])).astype(BF16)
        block_decay = jnp.exp(
            jnp.concatenate([c4, jnp.zeros((SUBLANES - 1, kw), F32)], axis=0))

        for hd in range(GLA_HEADS):
            kc = slice(hd * GLA_DK, (hd + 1) * GLA_DK)
            vcols = slice(hd * GLA_DV, (hd + 1) * GLA_DV)
            att = jnp.where(intra, dot_nt(q_e[:, kc], k_intra[:, kc]),
                            jnp.where(adjacent, dot_nt(q_e[:, kc], k_state[:, kc]), 0.0))
            att_far = jnp.where(far, dot_nt(q_far[:, kc], k_far[:, kc]), 0.0)
            att = jnp.concatenate(
                [att[:half],
                 jnp.concatenate([att[half:, :half] + att_far, att[half:, half:]], axis=1)],
                axis=0).astype(BF16)
            v_h = vv[rows, vcols]
            state = state_ref[hd]
            o = _dot(att, v_h) + _dot(q_blk[:, kc], state.astype(BF16))
            kv = lax.dot_general(k_blk[:, kc], v_h, (((0,), (0,)), ((), ())),
                                 preferred_element_type=F32)
            state_ref[hd] = state * block_decay[:, kc].T[:, 0:1] + kv
            o = _rmsnorm(o, gnorm_ref[...])
            og_ref[rows, vcols] = (o * jax.nn.silu(r[rows, vcols])).astype(BF16)

    gla_blocks = [functools.partial(gla_block, blk) for blk in range(n_blk)]

    def project_gla_out():
        t["y_gla"] = _dot(og_ref[...], w_o_gla_ref[...])

    def merge_and_project():
        merged = (t["gate_sgu"] * t["y_sgu"] + t["gate_gla"] * t["y_gla"]).astype(BF16)
        o_ref[...] = x + _dot(merged, w_out_ref[...])

    assert n_blk == 2
    for stage in (project_u, gelu_u, project_v, norm_v, project_tail, project_qk, log_decay,
                  project_vv, project_r, sgu, prefix_sums, gates, project_sgu_out,
                  gla_blocks[0], gla_blocks[1], project_gla_out, merge_and_project):
        stage()


def _ffn_kernel(x_ref, g_ref, w_up_ref, conv_w_ref, conv_b_ref, w_down_ref, final_g_ref,
                o_ref, hbuf_ref, *, final_norm):
    tb = x_ref.shape[0]
    pad = SUBLANES
    f = FFN_HIDDEN

    @pl.when(pl.program_id(1) == 0)
    def _():
        hbuf_ref[0:pad, :] = jnp.zeros((pad, 2 * f), F32)

    x = x_ref[...]
    hn = _rmsnorm(x, g_ref[...]).astype(BF16)

    hbuf_ref[pad:pad + tb, :] = _dot(hn, w_up_ref[...])
    hc = conv_b_ref[...]
    for j in range(CONV_WIDTH):
        shift = CONV_WIDTH - 1 - j
        hc = hc + hbuf_ref[pad - shift:pad - shift + tb, :] * conv_w_ref[j:j + 1, :]
    hbuf_ref[0:pad, :] = hbuf_ref[tb:tb + pad, :]
    a = hc[:, 0:f]
    b = hc[:, f:2 * f]
    y = x + _dot((jax.nn.silu(a) * b).astype(BF16), w_down_ref[...])
    if final_norm:
        y = _rmsnorm(y, final_g_ref[...])
    o_ref[...] = y


def _resident(shape):
    zeros = (0,) * len(shape)
    return pl.BlockSpec(shape, lambda b, s: zeros, pipeline_mode=pl.Buffered(1))


def _token_block(tb):
    return pl.BlockSpec((None, tb, D_MODEL), lambda b, s: (b, s, 0))


def _mixer_call(x, weights, layer, w_up, w_down, w_in_t, w_o_sgu, w_o_gla, w_out):
    batch, seq, _ = x.shape
    tb = MIXER_BLOCK
    steps = seq // tb
    assert seq % tb == 0 and tb % SGU_CHUNK == 0 and tb % (GLA_BLOCK_CHUNKS * GLA_CHUNK) == 0

    def cast_slices(w):
        rows, cols = w.shape[1:]
        assert rows % (steps * BF16_ROWS) == 0

        def slice_of(b, s):
            return jnp.minimum(b * steps + s, steps - 1)

        return (pl.BlockSpec((None, rows // steps, cols), lambda b, s: (layer, slice_of(b, s), 0)),
                pl.BlockSpec((rows // steps, cols), lambda b, s: (slice_of(b, s), 0)),
                jax.ShapeDtypeStruct((rows, cols), BF16))

    up_in, up_out, up_shape = cast_slices(w_up)
    down_in, down_out, down_shape = cast_slices(w_down)
    in_hbm = pl.BlockSpec(memory_space=pl.ANY)
    d = D_MODEL
    return pl.pallas_call(
        functools.partial(_mixer_kernel, layer=layer),
        grid=(batch, steps),
        in_specs=[_token_block(tb)] + [_resident(w.shape) for w in weights]
        + [up_in, down_in] + [in_hbm] * 4,
        out_specs=[_token_block(tb), up_out, down_out],
        out_shape=[jax.ShapeDtypeStruct(x.shape, x.dtype), up_shape, down_shape],
        scratch_shapes=[
            pltpu.VMEM((GLA_HEADS, GLA_DK, GLA_DV), F32),
            pltpu.VMEM((tb, SGU_WIDTH), BF16),
            pltpu.VMEM((tb, GLA_VAL_WIDTH), BF16),
            pltpu.VMEM((d, MAIN_COLS), BF16),
            pltpu.VMEM((d, TAIL_COLS), BF16),
            pltpu.VMEM((d, d), BF16),
            pltpu.VMEM((d, d), BF16),
            pltpu.VMEM((d, d), BF16),
            pltpu.VMEM((WEIGHT_STAGE_SLOTS, WEIGHT_STAGE_ROWS, d), F32),
            pltpu.SemaphoreType.DMA((WEIGHT_STAGE_SLOTS, 2)),
        ],
        compiler_params=pltpu.CompilerParams(
            dimension_semantics=("arbitrary", "arbitrary"),
            vmem_limit_bytes=VMEM_LIMIT_BYTES),
        name="mixer",
    )(x, *weights, w_up, w_down, w_in_t, w_o_sgu, w_o_gla, w_out)


def _ffn_call(x, weights, final_norm):
    batch, seq, _ = x.shape
    tb = FFN_BLOCK
    assert seq % tb == 0
    return pl.pallas_call(
        functools.partial(_ffn_kernel, final_norm=final_norm),
        grid=(batch, seq // tb),
        in_specs=[_token_block(tb)] + [_resident(w.shape) for w in weights],
        out_specs=_token_block(tb),
        out_shape=jax.ShapeDtypeStruct(x.shape, x.dtype),
        scratch_shapes=[pltpu.VMEM((tb + SUBLANES, 2 * FFN_HIDDEN), F32)],
        compiler_params=pltpu.CompilerParams(
            dimension_semantics=("arbitrary", "arbitrary"),
            vmem_limit_bytes=VMEM_LIMIT_BYTES),
        name="ffn_final" if final_norm else "ffn",
    )(x, *weights)


def _row(v):
    return v.reshape(1, -1).astype(F32)


def kernel(x, norm_mix_g, w_in, w_a2, b_a, sgu_w, sgu_b, sgu_ln_g, sgu_ln_b, gla_norm_g,
           w_o_sgu, w_o_gla, w_out, norm_ffn_g, w_up, conv_w, conv_b, w_down, final_norm_g):
    depth = w_in.shape[0]
    assert w_in.shape[2] == IN_COLS
    w_in_t = jnp.swapaxes(w_in, 1, 2)
    for l in range(depth):
        w_a2p = jnp.pad(w_a2[l], ((0, LANES - GLA_GATE_RANK), (0, 0)))
        sgu_bias = jnp.repeat(sgu_b[l].T, SGU_GROUP_DIM, axis=1)
        mixer_weights = (
            _row(norm_mix_g[l]),
            w_a2p.astype(BF16),
            _row(b_a[l]),
            sgu_w[l],
            sgu_bias,
            _row(sgu_ln_g[l]),
            _row(sgu_ln_b[l]),
            _row(gla_norm_g[l]),
        )
        x, w_up_l, w_down_l = _mixer_call(x, mixer_weights, l, w_up, w_down,
                                          w_in_t, w_o_sgu, w_o_gla, w_out)
        ffn_weights = (
            _row(norm_ffn_g[l]),
            w_up_l,
            conv_w[l],
            _row(conv_b[l]),
            w_down_l,
            _row(final_norm_g),
        )
        x = _ffn_call(x, ffn_weights, final_norm=(l == depth - 1))
    return x
```

```python
import functools

import jax
import jax.numpy as jnp
from jax import lax
from jax.experimental import pallas as pl
from jax.experimental.pallas import tpu as pltpu

D_MODEL = 1024
SGU_CHUNK = 128
SGU_GROUPS = 8
SGU_GROUP_DIM = 128
SGU_WIDTH = SGU_GROUPS * SGU_GROUP_DIM
GLA_HEADS = 4
GLA_DK = 128
GLA_DV = 256
GLA_KEY_WIDTH = GLA_HEADS * GLA_DK
GLA_VAL_WIDTH = GLA_HEADS * GLA_DV
GLA_GATE_RANK = 16
GLA_GATE_TEMP = 16.0
GLA_CHUNK = 64
FFN_HIDDEN = 2816
CONV_WIDTH = 3
EPS = 1e-6

LANES = 128
SUBLANES = 8
BF16_ROWS = 16
VMEM_LIMIT_BYTES = 58 * 1024 * 1024

OFF_ZU = 0
OFF_ZV = OFF_ZU + SGU_WIDTH
OFF_Q = OFF_ZV + SGU_WIDTH
OFF_K = OFF_Q + GLA_KEY_WIDTH
OFF_V = OFF_K + GLA_KEY_WIDTH
OFF_R = OFF_V + GLA_VAL_WIDTH
MAIN_COLS = OFF_R + GLA_VAL_WIDTH
IN_COLS = MAIN_COLS + GLA_GATE_RANK + 2 * D_MODEL
TAIL_ALIGNED = (IN_COLS - MAIN_COLS) // LANES * LANES
TAIL_COLS = TAIL_ALIGNED + LANES
GATE_SHIFT = GLA_GATE_RANK

MIXER_BLOCK = 512
FFN_BLOCK = 512
GLA_BLOCK_CHUNKS = 4
WEIGHT_STAGE_ROWS = 256
WEIGHT_STAGE_SLOTS = 5

F32 = jnp.float32
BF16 = jnp.bfloat16

def _dot(a, b):
    return jnp.dot(a, b, preferred_element_type=F32)


def _rmsnorm(x, g):
    return x * lax.rsqrt(jnp.mean(x * x, axis=-1, keepdims=True) + EPS) * g


def _split3(x):
    hi = x.astype(BF16)
    r1 = x - hi.astype(F32)
    mid = r1.astype(BF16)
    lo = (r1 - mid.astype(F32)).astype(BF16)
    return hi, mid, lo


def _load_mixer_weights(layer, w_in_t_hbm, w_o_sgu_hbm, w_o_gla_hbm, w_out_hbm,
                        w_main_ref, w_tail_ref, w_o_sgu_ref, w_o_gla_ref, w_out_ref,
                        stage_ref, sems):
    r = stage_ref.shape[1]
    d = D_MODEL
    jobs = []

    def transposed_into(col):
        def convert(slot_ref):
            block = slot_ref[...].T.astype(BF16)
            if col < MAIN_COLS:
                w_main_ref[:, col:col + r] = block
            else:
                w_tail_ref[:, col - MAIN_COLS:col - MAIN_COLS + r] = block
        return convert

    n_full = IN_COLS // r
    rest = IN_COLS - n_full * r
    assert MAIN_COLS % r == 0 and n_full * r == MAIN_COLS + TAIL_ALIGNED and rest <= LANES
    for j in range(n_full):
        jobs.append(([(w_in_t_hbm.at[layer, pl.ds(j * r, r), :], 0, r)],
                     transposed_into(j * r)))

    def tail_end(slot_ref):
        rows = jnp.concatenate(
            [slot_ref[0:rest, :], jnp.zeros((LANES - rest, d), F32)], axis=0)
        w_tail_ref[:, TAIL_ALIGNED:TAIL_COLS] = rows.T.astype(BF16)

    jobs.append(([(w_in_t_hbm.at[layer, pl.ds(n_full * r, rest), :], 0, rest)], tail_end))

    def rotated_rows_into(dst_ref, row):
        def convert(slot_ref):
            dst_ref[row:row + r, :] = pltpu.roll(slot_ref[...], GATE_SHIFT, axis=1).astype(BF16)
        return convert

    for src, dst in ((w_o_sgu_hbm, w_o_sgu_ref), (w_o_gla_hbm, w_o_gla_ref)):
        for row in range(0, d, r):
            jobs.append(([(src.at[layer, pl.ds(row, r), :], 0, r)],
                         rotated_rows_into(dst, row)))

    def rows_into(row):
        def convert(slot_ref):
            w_out_ref[row:row + r, :] = slot_ref[...].astype(BF16)
        return convert

    s = GATE_SHIFT
    jobs.append(([(w_out_hbm.at[layer, pl.ds(d - s, s), :], 0, s),
                  (w_out_hbm.at[layer, pl.ds(0, r - s), :], s, r - s)], rows_into(0)))
    for row in range(r, d, r):
        jobs.append(([(w_out_hbm.at[layer, pl.ds(row - s, r), :], 0, r)], rows_into(row)))

    n_slots = stage_ref.shape[0]

    def copies(i):
        slot = i % n_slots
        return [pltpu.make_async_copy(src, stage_ref.at[slot, pl.ds(off, n), :],
                                      sems.at[slot, p])
                for p, (src, off, n) in enumerate(jobs[i][0])]

    for i in range(min(n_slots - 1, len(jobs))):
        for c in copies(i):
            c.start()
    for i, (_, convert) in enumerate(jobs):
        ahead = i + n_slots - 1
        if ahead < len(jobs):
            for c in copies(ahead):
                c.start()
        for c in copies(i):
            c.wait()
        convert(stage_ref.at[i % n_slots])


def _mixer_kernel(x_ref, g_ref, w_a2_ref, b_a_ref,
                  sgu_w_ref, sgu_bias_ref, ln_g_ref, ln_b_ref, gnorm_ref,
                  w_up_src_ref, w_down_src_ref,
                  w_in_t_hbm, w_o_sgu_hbm, w_o_gla_hbm, w_out_hbm,
                  o_ref, w_up_dst_ref, w_down_dst_ref, state_ref, su_ref, og_ref,
                  w_main_ref, w_tail_ref, w_o_sgu_ref, w_o_gla_ref, w_out_ref,
                  stage_ref, sems, *, layer):
    tb = x_ref.shape[0]
    n_sgu = tb // SGU_CHUNK
    gb = GLA_BLOCK_CHUNKS * GLA_CHUNK
    n_blk = tb // gb
    half = gb // 2

    @pl.when(jnp.logical_and(pl.program_id(0) == 0, pl.program_id(1) == 0))
    def _():
        _load_mixer_weights(layer, w_in_t_hbm, w_o_sgu_hbm, w_o_gla_hbm, w_out_hbm,
                            w_main_ref, w_tail_ref, w_o_sgu_ref, w_o_gla_ref, w_out_ref,
                            stage_ref, sems)

    @pl.when(pl.program_id(1) == 0)
    def _():
        state_ref[...] = jnp.zeros_like(state_ref)

    w_up_dst_ref[...] = w_up_src_ref[...].astype(BF16)
    w_down_dst_ref[...] = w_down_src_ref[...].astype(BF16)

    x = x_ref[...]
    h = _rmsnorm(x, g_ref[...]).astype(BF16)

    def proj(off, width):
        return _dot(h, w_main_ref[:, off:off + width])

    kw = GLA_KEY_WIDTH
    t = {}

    def project_u():
        t["z_u"] = proj(OFF_ZU, SGU_WIDTH)

    def project_v():
        t["z_v"] = proj(OFF_ZV, SGU_WIDTH)

    def gelu_u():
        t["u"] = jax.nn.gelu(t["z_u"])

    def project_tail():
        t["tail"] = _dot(h, w_tail_ref[...])
        t["a_low"] = t["tail"][:, 0:LANES].astype(BF16)

    def norm_v():
        v = jax.nn.gelu(t["z_v"])
        mu = jnp.mean(v, axis=-1, keepdims=True)
        vc = v - mu
        var = jnp.mean(vc * vc, axis=-1, keepdims=True)
        t["vn"] = (vc * lax.rsqrt(var + EPS) * ln_g_ref[...] + ln_b_ref[...]).astype(BF16)

    def log_decay():
        z = _dot(t["a_low"], w_a2_ref[...]) + b_a_ref[...]
        t["log_a"] = jax.nn.log_sigmoid(z) * (1.0 / GLA_GATE_TEMP)

    ri = lax.broadcasted_iota(jnp.int32, (gb, gb), 0)
    ci = lax.broadcasted_iota(jnp.int32, (gb, gb), 1)
    chunk_gap = ri // GLA_CHUNK - ci // GLA_CHUNK
    intra = jnp.logical_and(chunk_gap == 0, ci <= ri)
    adjacent = chunk_gap == 1
    far = chunk_gap[half:, :half] >= 2

    def prefix_sums():
        prefix_sum = jnp.where(intra, 1.0, 0.0).astype(BF16)
        log_a_split = jnp.concatenate(_split3(t["log_a"]), axis=1)
        bcums = []
        for blk in range(n_blk):
            parts = _dot(prefix_sum, log_a_split[blk * gb:(blk + 1) * gb])
            bcums.append(parts[:, 0:kw] + parts[:, kw:2 * kw] + parts[:, 2 * kw:3 * kw])
        t["bcums"] = bcums

    def project_qk():
        t["q"] = proj(OFF_Q, GLA_KEY_WIDTH) * (GLA_DK ** -0.5)
        t["k"] = proj(OFF_K, GLA_KEY_WIDTH)

    def sgu():
        row = lax.broadcasted_iota(jnp.int32, (SGU_CHUNK, SGU_CHUNK), 0)
        col = lax.broadcasted_iota(jnp.int32, (SGU_CHUNK, SGU_CHUNK), 1)
        causal = row >= col
        vn, u = t["vn"], t["u"]
        for g in range(SGU_GROUPS):
            cols = slice(g * SGU_GROUP_DIM, (g + 1) * SGU_GROUP_DIM)
            wg = jnp.where(causal, sgu_w_ref[g], 0.0).astype(BF16)
            vg = jnp.concatenate(
                [vn[c * SGU_CHUNK:(c + 1) * SGU_CHUNK, cols] for c in range(n_sgu)], axis=1)
            svg = _dot(wg, vg)
            bias = sgu_bias_ref[:, cols]
            for c in range(n_sgu):
                rows = slice(c * SGU_CHUNK, (c + 1) * SGU_CHUNK)
                sv = svg[:, c * SGU_GROUP_DIM:(c + 1) * SGU_GROUP_DIM] + bias
                su_ref[rows, cols] = (u[rows, cols] * sv).astype(BF16)

    def project_vv():
        t["vv"] = proj(OFF_V, GLA_VAL_WIDTH).astype(BF16)

    def project_r():
        t["r"] = proj(OFF_R, GLA_VAL_WIDTH)

    def gates():
        tail = t["tail"]
        wrapped = lax.broadcasted_iota(jnp.int32, (tb, LANES), 1) < GATE_SHIFT

        def rotated_gate(base):
            first = jnp.where(wrapped, tail[:, base + D_MODEL:base + D_MODEL + LANES],
                              tail[:, base:base + LANES])
            return jnp.concatenate([first, tail[:, base + LANES:base + D_MODEL]], axis=1)

        t["gate_sgu"] = jax.nn.sigmoid(rotated_gate(0))
        t["gate_gla"] = jax.nn.sigmoid(rotated_gate(D_MODEL))

    def project_sgu_out():
        t["y_sgu"] = _dot(su_ref[...], w_o_sgu_ref[...])

    def per_chunk_rows(rows):
        return jnp.concatenate([jnp.broadcast_to(r, (GLA_CHUNK, kw)) for r in rows], axis=0)

    def dot_nt(a, b):
        return lax.dot_general(a, b, (((1,), (1,)), ((), ())), preferred_element_type=F32)

    def gla_block(blk):
        one = jnp.ones((1, kw), F32)
        vv, r = t["vv"], t["r"]
        rows = slice(blk * gb, (blk + 1) * gb)
        bcum = t["bcums"][blk]
        q_b = t["q"][rows]
        k_b = t["k"][rows]
        bl = [bcum[(c + 1) * GLA_CHUNK - 1:(c + 1) * GLA_CHUNK, :] for c in range(GLA_BLOCK_CHUNKS)]
        c2 = bl[0] + bl[1]
        c3 = c2 + bl[2]
        c4 = c3 + bl[3]

        q_e32 = q_b * jnp.exp(bcum)
        k_state32 = k_b * jnp.exp(per_chunk_rows(bl) - bcum)
        q_e = q_e32.astype(BF16)
        k_intra = (k_b * jnp.exp(-bcum)).astype(BF16)
        k_state = k_state32.astype(BF16)
        q_blk = (q_e32 * per_chunk_rows(
            [one, jnp.exp(bl[0]), jnp.exp(c2), jnp.exp(c3)])).astype(BF16)
        k_blk = (k_state32 * per_chunk_rows(
            [jnp.exp(bl[1] + bl[2] + bl[3]), jnp.exp(bl[2] + bl[3]), jnp.exp(bl[3]), one])
                 ).astype(BF16)
        q_far = (q_e32[half:] * per_chunk_rows([one, jnp.exp(bl[2])])).astype(BF16)
        k_far = (k_state32[:half] * per_chunk_rows([jnp.exp(bl[1]), one])).astype(BF16)
        block_decay = jnp.exp(
            jnp.concatenate([c4, jnp.zeros((SUBLANES - 1, kw), F32)], axis=0))

        for hd in range(GLA_HEADS):
            kc = slice(hd * GLA_DK, (hd + 1) * GLA_DK)
            vcols = slice(hd * GLA_DV, (hd + 1) * GLA_DV)
            att = jnp.where(intra, dot_nt(q_e[:, kc], k_intra[:, kc]),
                            jnp.where(adjacent, dot_nt(q_e[:, kc], k_state[:, kc]), 0.0))
            att_far = jnp.where(far, dot_nt(q_far[:, kc], k_far[:, kc]), 0.0)
            att = jnp.concatenate(
                [att[:half],
                 jnp.concatenate([att[half:, :half] + att_far, att[half:, half:]], axis=1)],
                axis=0).astype(BF16)
            v_h = vv[rows, vcols]
            state = state_ref[hd]
            o = _dot(att, v_h) + _dot(q_blk[:, kc], state.astype(BF16))
            kv = lax.dot_general(k_blk[:, kc], v_h, (((0,), (0,)), ((), ())),
                                 preferred_element_type=F32)
            state_ref[hd] = state * block_decay[:, kc].T[:, 0:1] + kv
            o = _rmsnorm(o, gnorm_ref[...])
            og_ref[rows, vcols] = (o * jax.nn.silu(r[rows, vcols])).astype(BF16)

    gla_blocks = [functools.partial(gla_block, blk) for blk in range(n_blk)]

    def project_gla_out():
        t["y_gla"] = _dot(og_ref[...], w_o_gla_ref[...])

    def merge_and_project():
        merged = (t["gate_sgu"] * t["y_sgu"] + t["gate_gla"] * t["y_gla"]).astype(BF16)
        o_ref[...] = x + _dot(merged, w_out_ref[...])

    assert n_blk == 2
    for stage in (project_u, gelu_u, project_v, norm_v, project_tail, project_qk, log_decay,
                  project_vv, project_r, sgu, prefix_sums, gates, project_sgu_out,
                  gla_blocks[0], gla_blocks[1], project_gla_out, merge_and_project):
        stage()


def _ffn_kernel(x_ref, g_ref, w_up_ref, conv_w_ref, conv_b_ref, w_down_ref, final_g_ref,
                o_ref, hbuf_ref, *, final_norm):
    tb = x_ref.shape[0]
    pad = SUBLANES
    f = FFN_HIDDEN

    @pl.when(pl.program_id(1) == 0)
    def _():
        hbuf_ref[0:pad, :] = jnp.zeros((pad, 2 * f), F32)

    x = x_ref[...]
    hn = _rmsnorm(x, g_ref[...]).astype(BF16)

    hbuf_ref[pad:pad + tb, :] = _dot(hn, w_up_ref[...])
    hc = conv_b_ref[...]
    for j in range(CONV_WIDTH):
        shift = CONV_WIDTH - 1 - j
        hc = hc + hbuf_ref[pad - shift:pad - shift + tb, :] * conv_w_ref[j:j + 1, :]
    hbuf_ref[0:pad, :] = hbuf_ref[tb:tb + pad, :]
    a = hc[:, 0:f]
    b = hc[:, f:2 * f]
    y = x + _dot((jax.nn.silu(a) * b).astype(BF16), w_down_ref[...])
    if final_norm:
        y = _rmsnorm(y, final_g_ref[...])
    o_ref[...] = y


def _resident(shape):
    zeros = (0,) * len(shape)
    return pl.BlockSpec(shape, lambda b, s: zeros, pipeline_mode=pl.Buffered(1))


def _token_block(tb):
    return pl.BlockSpec((None, tb, D_MODEL), lambda b, s: (b, s, 0))


def _mixer_call(x, weights, layer, w_up, w_down, w_in_t, w_o_sgu, w_o_gla, w_out):
    batch, seq, _ = x.shape
    tb = MIXER_BLOCK
    steps = seq // tb
    assert seq % tb == 0 and tb % SGU_CHUNK == 0 and tb % (GLA_BLOCK_CHUNKS * GLA_CHUNK) == 0

    def cast_slices(w):
        rows, cols = w.shape[1:]
        assert rows % (steps * BF16_ROWS) == 0

        def slice_of(b, s):
            return jnp.minimum(b * steps + s, steps - 1)

        return (pl.BlockSpec((None, rows // steps, cols), lambda b, s: (layer, slice_of(b, s), 0)),
                pl.BlockSpec((rows // steps, cols), lambda b, s: (slice_of(b, s), 0)),
                jax.ShapeDtypeStruct((rows, cols), BF16))

    up_in, up_out, up_shape = cast_slices(w_up)
    down_in, down_out, down_shape = cast_slices(w_down)
    in_hbm = pl.BlockSpec(memory_space=pl.ANY)
    d = D_MODEL
    return pl.pallas_call(
        functools.partial(_mixer_kernel, layer=layer),
        grid=(batch, steps),
        in_specs=[_token_block(tb)] + [_resident(w.shape) for w in weights]
        + [up_in, down_in] + [in_hbm] * 4,
        out_specs=[_token_block(tb), up_out, down_out],
        out_shape=[jax.ShapeDtypeStruct(x.shape, x.dtype), up_shape, down_shape],
        scratch_shapes=[
            pltpu.VMEM((GLA_HEADS, GLA_DK, GLA_DV), F32),
            pltpu.VMEM((tb, SGU_WIDTH), BF16),
            pltpu.VMEM((tb, GLA_VAL_WIDTH), BF16),
            pltpu.VMEM((d, MAIN_COLS), BF16),
            pltpu.VMEM((d, TAIL_COLS), BF16),
            pltpu.VMEM((d, d), BF16),
            pltpu.VMEM((d, d), BF16),
            pltpu.VMEM((d, d), BF16),
            pltpu.VMEM((WEIGHT_STAGE_SLOTS, WEIGHT_STAGE_ROWS, d), F32),
            pltpu.SemaphoreType.DMA((WEIGHT_STAGE_SLOTS, 2)),
        ],
        compiler_params=pltpu.CompilerParams(
            dimension_semantics=("arbitrary", "arbitrary"),
            vmem_limit_bytes=VMEM_LIMIT_BYTES),
        name="mixer",
    )(x, *weights, w_up, w_down, w_in_t, w_o_sgu, w_o_gla, w_out)


def _ffn_call(x, weights, final_norm):
    batch, seq, _ = x.shape
    tb = FFN_BLOCK
    assert seq % tb == 0
    return pl.pallas_call(
        functools.partial(_ffn_kernel, final_norm=final_norm),
        grid=(batch, seq // tb),
        in_specs=[_token_block(tb)] + [_resident(w.shape) for w in weights],
        out_specs=_token_block(tb),
        out_shape=jax.ShapeDtypeStruct(x.shape, x.dtype),
        scratch_shapes=[pltpu.VMEM((tb + SUBLANES, 2 * FFN_HIDDEN), F32)],
        compiler_params=pltpu.CompilerParams(
            dimension_semantics=("arbitrary", "arbitrary"),
            vmem_limit_bytes=VMEM_LIMIT_BYTES),
        name="ffn_final" if final_norm else "ffn",
    )(x, *weights)


def _row(v):
    return v.reshape(1, -1).astype(F32)


def kernel(x, norm_mix_g, w_in, w_a2, b_a, sgu_w, sgu_b, sgu_ln_g, sgu_ln_b, gla_norm_g,
           w_o_sgu, w_o_gla, w_out, norm_ffn_g, w_up, conv_w, conv_b, w_down, final_norm_g):
    depth = w_in.shape[0]
    assert w_in.shape[2] == IN_COLS
    w_in_t = jnp.swapaxes(w_in, 1, 2)
    for l in range(depth):
        w_a2p = jnp.pad(w_a2[l], ((0, LANES - GLA_GATE_RANK), (0, 0)))
        sgu_bias = jnp.repeat(sgu_b[l].T, SGU_GROUP_DIM, axis=1)
        mixer_weights = (
            _row(norm_mix_g[l]),
            w_a2p.astype(BF16),
            _row(b_a[l]),
            sgu_w[l],
            sgu_bias,
            _row(sgu_ln_g[l]),
            _row(sgu_ln_b[l]),
            _row(gla_norm_g[l]),
        )
        x, w_up_l, w_down_l = _mixer_call(x, mixer_weights, l, w_up, w_down,
                                          w_in_t, w_o_sgu, w_o_gla, w_out)
        ffn_weights = (
            _row(norm_ffn_g[l]),
            w_up_l,
            conv_w[l],
            _row(conv_b[l]),
            w_down_l,
            _row(final_norm_g),
        )
        x = _ffn_call(x, ffn_weights, final_norm=(l == depth - 1))
    return x
```

```python
import functools

import jax
import jax.numpy as jnp
from jax import lax
from jax.experimental import pallas as pl
from jax.experimental.pallas import tpu as pltpu

D_MODEL = 1024
SGU_CHUNK = 128
SGU_GROUPS = 8
SGU_GROUP_DIM = 128
SGU_WIDTH = SGU_GROUPS * SGU_GROUP_DIM
GLA_HEADS = 4
GLA_DK = 128
GLA_DV = 256
GLA_KEY_WIDTH = GLA_HEADS * GLA_DK
GLA_VAL_WIDTH = GLA_HEADS * GLA_DV
GLA_GATE_RANK = 16
GLA_GATE_TEMP = 16.0
GLA_CHUNK = 64
FFN_HIDDEN = 2816
CONV_WIDTH = 3
EPS = 1e-6

LANES = 128
SUBLANES = 8
BF16_ROWS = 16
VMEM_LIMIT_BYTES = 58 * 1024 * 1024

OFF_ZU = 0
OFF_ZV = OFF_ZU + SGU_WIDTH
OFF_Q = OFF_ZV + SGU_WIDTH
OFF_K = OFF_Q + GLA_KEY_WIDTH
OFF_V = OFF_K + GLA_KEY_WIDTH
OFF_R = OFF_V + GLA_VAL_WIDTH
MAIN_COLS = OFF_R + GLA_VAL_WIDTH
IN_COLS = MAIN_COLS + GLA_GATE_RANK + 2 * D_MODEL
TAIL_ALIGNED = (IN_COLS - MAIN_COLS) // LANES * LANES
TAIL_COLS = TAIL_ALIGNED + LANES
GATE_SHIFT = GLA_GATE_RANK

MIXER_BLOCK = 512
FFN_BLOCK = 512
GLA_BLOCK_CHUNKS = 4
FFN_DOWN_SPLITS = (0, 1024, 2048, FFN_HIDDEN)
WEIGHT_STAGE_ROWS = 256
WEIGHT_STAGE_SLOTS = 5

F32 = jnp.float32
BF16 = jnp.bfloat16

def _dot(a, b):
    return jnp.dot(a, b, preferred_element_type=F32)


def _rmsnorm(x, g):
    return x * lax.rsqrt(jnp.mean(x * x, axis=-1, keepdims=True) + EPS) * g


def _split3(x):
    hi = x.astype(BF16)
    r1 = x - hi.astype(F32)
    mid = r1.astype(BF16)
    lo = (r1 - mid.astype(F32)).astype(BF16)
    return hi, mid, lo


def _load_mixer_weights(layer, w_in_t_hbm, w_o_sgu_hbm, w_o_gla_hbm, w_out_hbm,
                        w_main_ref, w_tail_ref, w_o_sgu_ref, w_o_gla_ref, w_out_ref,
                        stage_ref, sems):
    r = stage_ref.shape[1]
    d = D_MODEL
    jobs = []

    def transposed_into(col):
        def convert(slot_ref):
            block = slot_ref[...].T.astype(BF16)
            if col < MAIN_COLS:
                w_main_ref[:, col:col + r] = block
            else:
                w_tail_ref[:, col - MAIN_COLS:col - MAIN_COLS + r] = block
        return convert

    n_full = IN_COLS // r
    rest = IN_COLS - n_full * r
    assert MAIN_COLS % r == 0 and n_full * r == MAIN_COLS + TAIL_ALIGNED and rest <= LANES
    for j in range(n_full):
        jobs.append(([(w_in_t_hbm.at[layer, pl.ds(j * r, r), :], 0, r)],
                     transposed_into(j * r)))

    def tail_end(slot_ref):
        rows = jnp.concatenate(
            [slot_ref[0:rest, :], jnp.zeros((LANES - rest, d), F32)], axis=0)
        w_tail_ref[:, TAIL_ALIGNED:TAIL_COLS] = rows.T.astype(BF16)

    jobs.append(([(w_in_t_hbm.at[layer, pl.ds(n_full * r, rest), :], 0, rest)], tail_end))

    def rotated_rows_into(dst_ref, row):
        def convert(slot_ref):
            dst_ref[row:row + r, :] = pltpu.roll(slot_ref[...], GATE_SHIFT, axis=1).astype(BF16)
        return convert

    for src, dst in ((w_o_sgu_hbm, w_o_sgu_ref), (w_o_gla_hbm, w_o_gla_ref)):
        for row in range(0, d, r):
            jobs.append(([(src.at[layer, pl.ds(row, r), :], 0, r)],
                         rotated_rows_into(dst, row)))

    def rows_into(row):
        def convert(slot_ref):
            w_out_ref[row:row + r, :] = slot_ref[...].astype(BF16)
        return convert

    s = GATE_SHIFT
    jobs.append(([(w_out_hbm.at[layer, pl.ds(d - s, s), :], 0, s),
                  (w_out_hbm.at[layer, pl.ds(0, r - s), :], s, r - s)], rows_into(0)))
    for row in range(r, d, r):
        jobs.append(([(w_out_hbm.at[layer, pl.ds(row - s, r), :], 0, r)], rows_into(row)))

    n_slots = stage_ref.shape[0]

    def copies(i):
        slot = i % n_slots
        return [pltpu.make_async_copy(src, stage_ref.at[slot, pl.ds(off, n), :],
                                      sems.at[slot, p])
                for p, (src, off, n) in enumerate(jobs[i][0])]

    for i in range(min(n_slots - 1, len(jobs))):
        for c in copies(i):
            c.start()
    for i, (_, convert) in enumerate(jobs):
        ahead = i + n_slots - 1
        if ahead < len(jobs):
            for c in copies(ahead):
                c.start()
        for c in copies(i):
            c.wait()
        convert(stage_ref.at[i % n_slots])


def _mixer_kernel(x_ref, g_ref, w_a2_ref, b_a_ref,
                  sgu_w_ref, sgu_bias_ref, ln_g_ref, ln_b_ref, gnorm_ref,
                  w_up_src_ref, w_down_src_ref,
                  w_in_t_hbm, w_o_sgu_hbm, w_o_gla_hbm, w_out_hbm,
                  o_ref, w_up_dst_ref, w_down_dst_ref, state_ref, su_ref, og_ref,
                  w_main_ref, w_tail_ref, w_o_sgu_ref, w_o_gla_ref, w_out_ref,
                  stage_ref, sems, *, layer):
    tb = x_ref.shape[0]
    n_sgu = tb // SGU_CHUNK
    gb = GLA_BLOCK_CHUNKS * GLA_CHUNK
    n_blk = tb // gb
    half = gb // 2

    @pl.when(jnp.logical_and(pl.program_id(0) == 0, pl.program_id(1) == 0))
    def _():
        _load_mixer_weights(layer, w_in_t_hbm, w_o_sgu_hbm, w_o_gla_hbm, w_out_hbm,
                            w_main_ref, w_tail_ref, w_o_sgu_ref, w_o_gla_ref, w_out_ref,
                            stage_ref, sems)

    @pl.when(pl.program_id(1) == 0)
    def _():
        state_ref[...] = jnp.zeros_like(state_ref)

    w_up_dst_ref[...] = w_up_src_ref[...].astype(BF16)
    w_down_dst_ref[...] = w_down_src_ref[...].astype(BF16)

    h = _rmsnorm(x_ref[...], g_ref[...]).astype(BF16)

    def proj(off, width):
        return _dot(h, w_main_ref[:, off:off + width])

    kw = GLA_KEY_WIDTH
    t = {}

    def project_u():
        t["z_u"] = proj(OFF_ZU, SGU_WIDTH)

    def project_v():
        t["z_v"] = proj(OFF_ZV, SGU_WIDTH)

    def gelu_u():
        t["u"] = jax.nn.gelu(t["z_u"])

    def project_tail():
        t["tail"] = _dot(h, w_tail_ref[...])
        t["a_low"] = t["tail"][:, 0:LANES].astype(BF16)

    def norm_v():
        v = jax.nn.gelu(t["z_v"])
        mu = jnp.mean(v, axis=-1, keepdims=True)
        vc = v - mu
        var = jnp.mean(vc * vc, axis=-1, keepdims=True)
        t["vn"] = (vc * lax.rsqrt(var + EPS) * ln_g_ref[...] + ln_b_ref[...]).astype(BF16)

    def log_decay():
        z = _dot(t["a_low"], w_a2_ref[...]) + b_a_ref[...]
        t["log_a"] = jax.nn.log_sigmoid(z) * (1.0 / GLA_GATE_TEMP)

    ri = lax.broadcasted_iota(jnp.int32, (gb, gb), 0)
    ci = lax.broadcasted_iota(jnp.int32, (gb, gb), 1)
    chunk_gap = ri // GLA_CHUNK - ci // GLA_CHUNK
    intra = jnp.logical_and(chunk_gap == 0, ci <= ri)
    adjacent = chunk_gap == 1
    far = chunk_gap[half:, :half] >= 2

    def prefix_sums():
        prefix_sum = jnp.where(intra, 1.0, 0.0).astype(BF16)
        log_a_split = jnp.concatenate(_split3(t["log_a"]), axis=1)
        bcums = []
        for blk in range(n_blk):
            parts = _dot(prefix_sum, log_a_split[blk * gb:(blk + 1) * gb])
            bcums.append(parts[:, 0:kw] + parts[:, kw:2 * kw] + parts[:, 2 * kw:3 * kw])
        t["bcums"] = bcums

    def project_qk():
        t["q"] = proj(OFF_Q, GLA_KEY_WIDTH) * (GLA_DK ** -0.5)
        t["k"] = proj(OFF_K, GLA_KEY_WIDTH)

    def sgu():
        row = lax.broadcasted_iota(jnp.int32, (SGU_CHUNK, SGU_CHUNK), 0)
        col = lax.broadcasted_iota(jnp.int32, (SGU_CHUNK, SGU_CHUNK), 1)
        causal = row >= col
        vn, u = t["vn"], t["u"]
        for g in range(SGU_GROUPS):
            cols = slice(g * SGU_GROUP_DIM, (g + 1) * SGU_GROUP_DIM)
            wg = jnp.where(causal, sgu_w_ref[g], 0.0).astype(BF16)
            vg = jnp.concatenate(
                [vn[c * SGU_CHUNK:(c + 1) * SGU_CHUNK, cols] for c in range(n_sgu)], axis=1)
            svg = _dot(wg, vg)
            bias = sgu_bias_ref[:, cols]
            for c in range(n_sgu):
                rows = slice(c * SGU_CHUNK, (c + 1) * SGU_CHUNK)
                sv = svg[:, c * SGU_GROUP_DIM:(c + 1) * SGU_GROUP_DIM] + bias
                su_ref[rows, cols] = (u[rows, cols] * sv).astype(BF16)

    def project_vv():
        t["vv"] = proj(OFF_V, GLA_VAL_WIDTH).astype(BF16)

    def project_r():
        t["r"] = proj(OFF_R, GLA_VAL_WIDTH)

    def gates():
        tail = t["tail"]
        wrapped = lax.broadcasted_iota(jnp.int32, (tb, LANES), 1) < GATE_SHIFT

        def rotated_gate(base):
            first = jnp.where(wrapped, tail[:, base + D_MODEL:base + D_MODEL + LANES],
                              tail[:, base:base + LANES])
            return jnp.concatenate([first, tail[:, base + LANES:base + D_MODEL]], axis=1)

        t["gate_sgu"] = jax.nn.sigmoid(rotated_gate(0))
        t["gate_gla"] = jax.nn.sigmoid(rotated_gate(D_MODEL))

    def project_sgu_out():
        t["y_sgu"] = _dot(su_ref[...], w_o_sgu_ref[...])

    def per_chunk_rows(rows):
        return jnp.concatenate([jnp.broadcast_to(r, (GLA_CHUNK, kw)) for r in rows], axis=0)

    def dot_nt(a, b):
        return lax.dot_general(a, b, (((1,), (1,)), ((), ())), preferred_element_type=F32)

    def gla_block(blk):
        one = jnp.ones((1, kw), F32)
        vv, r = t["vv"], t["r"]
        rows = slice(blk * gb, (blk + 1) * gb)
        bcum = t["bcums"][blk]
        q_b = t["q"][rows]
        k_b = t["k"][rows]
        bl = [bcum[(c + 1) * GLA_CHUNK - 1:(c + 1) * GLA_CHUNK, :] for c in range(GLA_BLOCK_CHUNKS)]
        c2 = bl[0] + bl[1]
        c3 = c2 + bl[2]
        c4 = c3 + bl[3]

        q_e32 = q_b * jnp.exp(bcum)
        k_state32 = k_b * jnp.exp(per_chunk_rows(bl) - bcum)
        q_e = q_e32.astype(BF16)
        k_intra = (k_b * jnp.exp(-bcum)).astype(BF16)
        k_state = k_state32.astype(BF16)
        q_blk = (q_e32 * per_chunk_rows(
            [one, jnp.exp(bl[0]), jnp.exp(c2), jnp.exp(c3)])).astype(BF16)
        k_blk = (k_state32 * per_chunk_rows(
            [jnp.exp(bl[1] + bl[2] + bl[3]), jnp.exp(bl[2] + bl[3]), jnp.exp(bl[3]), one])
                 ).astype(BF16)
        q_far = (q_e32[half:] * per_chunk_rows([one, jnp.exp(bl[2])])).astype(BF16)
        k_far = (k_state32[:half] * per_chunk_rows([jnp.exp(bl[1]), one])).astype(BF16)
        block_decay = jnp.exp(
            jnp.concatenate([c4, jnp.zeros((SUBLANES - 1, kw), F32)], axis=0))

        for hd in range(GLA_HEADS):
            kc = slice(hd * GLA_DK, (hd + 1) * GLA_DK)
            vcols = slice(hd * GLA_DV, (hd + 1) * GLA_DV)
            att = jnp.where(intra, dot_nt(q_e[:, kc], k_intra[:, kc]),
                            jnp.where(adjacent, dot_nt(q_e[:, kc], k_state[:, kc]), 0.0))
            att_far = jnp.where(far, dot_nt(q_far[:, kc], k_far[:, kc]), 0.0)
            att = jnp.concatenate(
                [att[:half],
                 jnp.concatenate([att[half:, :half] + att_far, att[half:, half:]], axis=1)],
                axis=0).astype(BF16)
            v_h = vv[rows, vcols]
            state = state_ref[hd]
            o = _dot(att, v_h) + _dot(q_blk[:, kc], state.astype(BF16))
            kv = lax.dot_general(k_blk[:, kc], v_h, (((0,), (0,)), ((), ())),
                                 preferred_element_type=F32)
            state_ref[hd] = state * block_decay[:, kc].T[:, 0:1] + kv
            o = _rmsnorm(o, gnorm_ref[...])
            og_ref[rows, vcols] = (o * jax.nn.silu(r[rows, vcols])).astype(BF16)

    gla_blocks = [functools.partial(gla_block, blk) for blk in range(n_blk)]

    def project_gla_out():
        t["y_gla"] = _dot(og_ref[...], w_o_gla_ref[...])

    def merge_and_project():
        merged = (t["gate_sgu"] * t["y_sgu"] + t["gate_gla"] * t["y_gla"]).astype(BF16)
        o_ref[...] = x_ref[...] + _dot(merged, w_out_ref[...])

    assert n_blk == 2
    for stage in (project_u, gelu_u, project_v, norm_v, project_tail, project_qk, log_decay,
                  project_vv, project_r, sgu, prefix_sums, gates, project_sgu_out,
                  gla_blocks[0], gla_blocks[1], project_gla_out, merge_and_project):
        stage()


def _ffn_kernel(x_ref, g_ref, w_up_ref, conv_w_ref, conv_b_ref, w_down_ref, final_g_ref,
                o_ref, hbuf_ref, *, final_norm):
    tb = x_ref.shape[0]
    pad = SUBLANES
    f = FFN_HIDDEN

    @pl.when(pl.program_id(1) == 0)
    def _():
        hbuf_ref[0:pad, :] = jnp.zeros((pad, 2 * f), F32)

    hn = _rmsnorm(x_ref[...], g_ref[...]).astype(BF16)

    hbuf_ref[pad:pad + tb, :] = _dot(hn, w_up_ref[...])
    hc = conv_b_ref[...]
    for j in range(CONV_WIDTH):
        shift = CONV_WIDTH - 1 - j
        hc = hc + hbuf_ref[pad - shift:pad - shift + tb, :] * conv_w_ref[j:j + 1, :]
    hbuf_ref[0:pad, :] = hbuf_ref[tb:tb + pad, :]
    a = hc[:, 0:f]
    b = hc[:, f:2 * f]
    y = x_ref[...]
    for lo, hi in zip(FFN_DOWN_SPLITS[:-1], FFN_DOWN_SPLITS[1:]):
        gated = (jax.nn.silu(a[:, lo:hi]) * b[:, lo:hi]).astype(BF16)
        y = y + _dot(gated, w_down_ref[lo:hi, :])
    if final_norm:
        y = _rmsnorm(y, final_g_ref[...])
    o_ref[...] = y


def _resident(shape):
    zeros = (0,) * len(shape)
    return pl.BlockSpec(shape, lambda b, s: zeros, pipeline_mode=pl.Buffered(1))


def _token_block(tb):
    return pl.BlockSpec((None, tb, D_MODEL), lambda b, s: (b, s, 0))


def _mixer_call(x, weights, layer, w_up, w_down, w_in_t, w_o_sgu, w_o_gla, w_out):
    batch, seq, _ = x.shape
    tb = MIXER_BLOCK
    steps = seq // tb
    assert seq % tb == 0 and tb % SGU_CHUNK == 0 and tb % (GLA_BLOCK_CHUNKS * GLA_CHUNK) == 0

    def cast_slices(w):
        rows, cols = w.shape[1:]
        assert rows % (steps * BF16_ROWS) == 0

        def slice_of(b, s):
            return jnp.minimum(b * steps + s, steps - 1)

        return (pl.BlockSpec((None, rows // steps, cols), lambda b, s: (layer, slice_of(b, s), 0)),
                pl.BlockSpec((rows // steps, cols), lambda b, s: (slice_of(b, s), 0)),
                jax.ShapeDtypeStruct((rows, cols), BF16))

    up_in, up_out, up_shape = cast_slices(w_up)
    down_in, down_out, down_shape = cast_slices(w_down)
    in_hbm = pl.BlockSpec(memory_space=pl.ANY)
    d = D_MODEL
    return pl.pallas_call(
        functools.partial(_mixer_kernel, layer=layer),
        grid=(batch, steps),
        in_specs=[_token_block(tb)] + [_resident(w.shape) for w in weights]
        + [up_in, down_in] + [in_hbm] * 4,
        out_specs=[_token_block(tb), up_out, down_out],
        out_shape=[jax.ShapeDtypeStruct(x.shape, x.dtype), up_shape, down_shape],
        scratch_shapes=[
            pltpu.VMEM((GLA_HEADS, GLA_DK, GLA_DV), F32),
            pltpu.VMEM((tb, SGU_WIDTH), BF16),
            pltpu.VMEM((tb, GLA_VAL_WIDTH), BF16),
            pltpu.VMEM((d, MAIN_COLS), BF16),
            pltpu.VMEM((d, TAIL_COLS), BF16),
            pltpu.VMEM((d, d), BF16),
            pltpu.VMEM((d, d), BF16),
            pltpu.VMEM((d, d), BF16),
            pltpu.VMEM((WEIGHT_STAGE_SLOTS, WEIGHT_STAGE_ROWS, d), F32),
            pltpu.SemaphoreType.DMA((WEIGHT_STAGE_SLOTS, 2)),
        ],
        compiler_params=pltpu.CompilerParams(
            dimension_semantics=("arbitrary", "arbitrary"),
            vmem_limit_bytes=VMEM_LIMIT_BYTES),
        name="mixer",
    )(x, *weights, w_up, w_down, w_in_t, w_o_sgu, w_o_gla, w_out)


def _ffn_call(x, weights, final_norm):
    batch, seq, _ = x.shape
    tb = FFN_BLOCK
    assert seq % tb == 0
    return pl.pallas_call(
        functools.partial(_ffn_kernel, final_norm=final_norm),
        grid=(batch, seq // tb),
        in_specs=[_token_block(tb)] + [_resident(w.shape) for w in weights],
        out_specs=_token_block(tb),
        out_shape=jax.ShapeDtypeStruct(x.shape, x.dtype),
        scratch_shapes=[pltpu.VMEM((tb + SUBLANES, 2 * FFN_HIDDEN), F32)],
        compiler_params=pltpu.CompilerParams(
            dimension_semantics=("arbitrary", "arbitrary"),
            vmem_limit_bytes=VMEM_LIMIT_BYTES),
        name="ffn_final" if final_norm else "ffn",
    )(x, *weights)


def _row(v):
    return v.reshape(1, -1).astype(F32)


def kernel(x, norm_mix_g, w_in, w_a2, b_a, sgu_w, sgu_b, sgu_ln_g, sgu_ln_b, gla_norm_g,
           w_o_sgu, w_o_gla, w_out, norm_ffn_g, w_up, conv_w, conv_b, w_down, final_norm_g):
    depth = w_in.shape[0]
    assert w_in.shape[2] == IN_COLS
    w_in_t = jnp.swapaxes(w_in, 1, 2)
    for l in range(depth):
        w_a2p = jnp.pad(w_a2[l], ((0, LANES - GLA_GATE_RANK), (0, 0)))
        sgu_bias = jnp.repeat(sgu_b[l].T, SGU_GROUP_DIM, axis=1)
        mixer_weights = (
            _row(norm_mix_g[l]),
            w_a2p.astype(BF16),
            _row(b_a[l]),
            sgu_w[l],
            sgu_bias,
            _row(sgu_ln_g[l]),
            _row(sgu_ln_b[l]),
            _row(gla_norm_g[l]),
        )
        x, w_up_l, w_down_l = _mixer_call(x, mixer_weights, l, w_up, w_down,
                                          w_in_t, w_o_sgu, w_o_gla, w_out)
        ffn_weights = (
            _row(norm_ffn_g[l]),
            w_up_l,
            conv_w[l],
            _row(conv_b[l]),
            w_down_l,
            _row(final_norm_g),
        )
        x = _ffn_call(x, ffn_weights, final_norm=(l == depth - 1))
    return x
```
